```python
import math
import jax, jax.numpy as jnp
from jax import lax
import numpy as np

D_MODEL = 4096
BATCH = 4
SEQ = 2048
DEPTH = 4
DEC_BATCH = 8
DEC_SEQ = 4
PAST_LEN = 8192
PAGE_SIZE = 128

HEAD_DIM = 128
N_BRANCH = 4
MIX_WIDTH = D_MODEL // N_BRANCH
N_HEADS = MIX_WIDTH // HEAD_DIM
D_FF = 4 * D_MODEL
EPS = 1e-6
NEG = -1e30
Q_BLOCK = 128

GDN_CONV = 4
GDN_CHUNK = 64

RET_CHUNK = 64
ROPE_BASE = 10000.0

NSA_KV_HEADS = 2
NSA_GROUP = N_HEADS // NSA_KV_HEADS
CMP_BLOCK = 32
CMP_STRIDE = 16
SEL_BLOCK = 64
SEL_TOPN = 16
WINDOW = 512
FORCE_BONUS = 1e4

DSA_KV_HEADS = 2
DSA_GROUP = N_HEADS // DSA_KV_HEADS
IDX_HEADS = 16
IDX_DIM = 64
DSA_TOPK_MAX = 256

KVW = NSA_KV_HEADS * HEAD_DIM
DKVW = DSA_KV_HEADS * HEAD_DIM
PAGED_SIZES = (KVW, KVW, KVW, KVW, DKVW, DKVW, IDX_DIM)
PAGED_W = 4 * KVW + 2 * DKVW + IDX_DIM
WIN_W = 2 * KVW
IN_SIZES = (
    3 * MIX_WIDTH, MIX_WIDTH, N_HEADS, N_HEADS,
    MIX_WIDTH, MIX_WIDTH, MIX_WIDTH, MIX_WIDTH,
    MIX_WIDTH, KVW, KVW, KVW, KVW, KVW, KVW, 3 * N_HEADS,
    MIX_WIDTH, DKVW, DKVW, IDX_HEADS * IDX_DIM, IDX_DIM, IDX_HEADS,
    N_BRANCH * D_MODEL)
IN_WIDTH = sum(IN_SIZES)

kernel_name = 'hybrid_gdn_retnet_nsa_dsa_step'


def _rmsnorm(x, g):
    xf = x.astype(jnp.float32)
    y = xf * lax.rsqrt(jnp.mean(xf * xf, axis=-1, keepdims=True) + EPS)
    return (y * g.astype(jnp.float32)).astype(x.dtype)


def _l2norm(x):
    return x * lax.rsqrt(jnp.sum(x * x, axis=-1, keepdims=True) + EPS)


def _split(h, sizes):
    return jnp.split(h, np.cumsum(sizes)[:-1].tolist(), axis=-1)


def _chunk_len(T, c):
    return c if T % c == 0 else T


def _rope(x, pos):
    half = x.shape[-1] // 2
    freq = jnp.exp(-math.log(ROPE_BASE) * jnp.arange(half, dtype=jnp.float32) / half)
    ang = pos.astype(jnp.float32)[:, None] * freq
    cos, sin = jnp.cos(ang)[None, :, None, :], jnp.sin(ang)[None, :, None, :]
    x1, x2 = x[..., :half], x[..., half:]
    return jnp.concatenate([x1 * cos - x2 * sin, x1 * sin + x2 * cos], axis=-1)


def _over_query_blocks(fn, q_args, q_pos):
    T = q_pos.shape[0]
    if T <= Q_BLOCK or T % Q_BLOCK:
        return fn(*q_args, q_pos)
    nb = T // Q_BLOCK
    split = lambda a: jnp.moveaxis(a.reshape(a.shape[0], nb, Q_BLOCK, *a.shape[2:]), 1, 0)
    xs = tuple(split(a) for a in q_args) + (q_pos.reshape(nb, Q_BLOCK),)
    out = jnp.moveaxis(lax.map(lambda z: fn(*z), xs), 0, 1)
    return out.reshape(out.shape[0], T, *out.shape[3:])


def _attend_shared(q, k, v, mask):
    s = jnp.einsum('bqgrd,bkgd->bqgrk', q, k) * HEAD_DIM ** -0.5
    p = jax.nn.softmax(jnp.where(mask, s, NEG), axis=-1)
    return jnp.einsum('bqgrk,bkgd->bqgrd', p, v)


def _attend_gathered(q, k, v, mask):
    s = jnp.einsum('bqgrd,bqgmd->bqgrm', q, k) * HEAD_DIM ** -0.5
    p = jax.nn.softmax(jnp.where(mask, s, NEG), axis=-1)
    return jnp.einsum('bqgrm,bqgmd->bqgrd', p, v)


def _gated_delta_rule(q, k, v, g, beta, S0):
    B, T, H, dk = q.shape
    dv = v.shape[-1]
    C = _chunk_len(T, GDN_CHUNK)
    N = T // C
    ch = lambda a: jnp.swapaxes(a.reshape(B, N, C, H, *a.shape[3:]), 2, 3)
    q, k, v, g, beta = ch(q), ch(k), ch(v), ch(g), ch(beta)
    gc = jnp.cumsum(g, axis=-1)
    incl = jnp.tril(jnp.ones((C, C), bool))
    strict = jnp.tril(jnp.ones((C, C), jnp.float32), -1)
    decay = jnp.exp(jnp.where(incl, gc[..., :, None] - gc[..., None, :], -jnp.inf))
    kb = k * beta[..., None]
    a_mat = jnp.einsum('bnhid,bnhjd->bnhij', kb, k) * decay * strict
    eye = jnp.eye(C, dtype=jnp.float32)
    t_mat = lax.linalg.triangular_solve(a_mat + eye, jnp.broadcast_to(eye, a_mat.shape), left_side=True, lower=True)
    u = jnp.einsum('bnhij,bnhjd->bnhid', t_mat, v * beta[..., None])
    w = jnp.einsum('bnhij,bnhjd->bnhid', t_mat, kb * jnp.exp(gc)[..., None])
    qk = jnp.einsum('bnhid,bnhjd->bnhij', q, k) * decay
    qg = q * jnp.exp(gc)[..., None]
    kd = k * jnp.exp(gc[..., -1:] - gc)[..., None]
    glast = jnp.exp(gc[..., -1])

    def step(S, xs):
        u_i, w_i, qk_i, qg_i, kd_i, gl_i = xs
        v_new = u_i - jnp.einsum('bhcd,bhde->bhce', w_i, S)
        o = jnp.einsum('bhcd,bhde->bhce', qg_i, S) + jnp.einsum('bhij,bhje->bhie', qk_i, v_new)
        S = S * gl_i[..., None, None] + jnp.einsum('bhcd,bhce->bhde', kd_i, v_new)
        return S, o

    xs = tuple(jnp.moveaxis(a, 1, 0) for a in (u, w, qk, qg, kd, glast))
    S, o = lax.scan(step, S0, xs)
    return o.transpose(1, 0, 3, 2, 4).reshape(B, T, H, dv), S


def _gdn(qkv, z, b_raw, a_raw, conv_buf, S0, conv_w, a_log, dt_bias, norm_g):
    B, T, _ = qkv.shape
    xc = jnp.concatenate([conv_buf, qkv], axis=1)
    y = jax.nn.silu(sum(xc[:, j:j + T] * conv_w[j] for j in range(GDN_CONV)))
    q, k, v = [a.reshape(B, T, N_HEADS, HEAD_DIM) for a in jnp.split(y, 3, axis=-1)]
    q = _l2norm(q) * HEAD_DIM ** -0.5
    k = _l2norm(k)
    beta = jax.nn.sigmoid(b_raw)
    g = -jnp.exp(a_log) * jax.nn.softplus(a_raw + dt_bias)
    o, S = _gated_delta_rule(q, k, v, g, beta, S0)
    o = _rmsnorm(o, norm_g) * jax.nn.silu(z.reshape(B, T, N_HEADS, HEAD_DIM))
    return o.reshape(B, T, MIX_WIDTH), xc[:, T:], S


def _retention(q, k, v, gate, pos, S0, gn_g):
    B, T, _ = q.shape
    H, d = N_HEADS, HEAD_DIM
    q = _rope(q.reshape(B, T, H, d), pos)
    k = _rope(k.reshape(B, T, H, d), pos) * d ** -0.5
    v = v.reshape(B, T, H, d)
    lg = jnp.log1p(-jnp.exp2(-5.0 - jnp.arange(H, dtype=jnp.float32)))
    C = _chunk_len(T, RET_CHUNK)
    N = T // C
    qc, kc, vc = (a.reshape(B, N, C, H, d) for a in (q, k, v))
    i = jnp.arange(C, dtype=jnp.float32)
    diff = i[:, None] - i[None, :]
    dmat = jnp.where(diff >= 0, jnp.exp(jnp.maximum(diff, 0.0)[None] * lg[:, None, None]), 0.0)
    o_in = jnp.einsum('bnhij,bnjhe->bnihe', jnp.einsum('bnihd,bnjhd->bnhij', qc, kc) * dmat, vc)
    q_dec = qc * jnp.exp((i + 1.0)[:, None] * lg)[:, :, None]
    k_dec = kc * jnp.exp((C - 1.0 - i)[:, None] * lg)[:, :, None]
    kv = jnp.einsum('bnjhd,bnjhe->bnhde', k_dec, vc)
    chunk_decay = jnp.exp(C * lg)[:, None, None]

    def step(S, xs):
        qd_i, kv_i = xs
        o = jnp.einsum('bihd,bhde->bihe', qd_i, S)
        return S * chunk_decay + kv_i, o

    S, o_x = lax.scan(step, S0, (jnp.moveaxis(q_dec, 1, 0), jnp.moveaxis(kv, 1, 0)))
    o = (o_in + jnp.moveaxis(o_x, 0, 1)).reshape(B, T, H, d)
    oc = o - jnp.mean(o, axis=-1, keepdims=True)
    o = oc * lax.rsqrt(jnp.mean(oc * oc, axis=-1, keepdims=True) + EPS)
    return jax.nn.silu(gate) * (o.reshape(B, T, MIX_WIDTH) * gn_g), S


def _compress(k, w, pe):
    L = k.shape[1]
    n_cmp = (L - CMP_BLOCK) // CMP_STRIDE + 1
    idx = jnp.arange(n_cmp)[:, None] * CMP_STRIDE + jnp.arange(CMP_BLOCK)[None, :]
    blk = k[:, idx] + pe[None, None, :, None, :]
    return jnp.einsum('bncgd,cde->bnge', blk, w)


def _nsa_compressed(qg, q_pos, kc, vc, wk, wv, pek, pev):
    ck, cv = _compress(kc, wk, pek), _compress(vc, wv, pev)
    n_cmp = ck.shape[1]
    vis = (jnp.arange(n_cmp) * CMP_STRIDE + CMP_BLOCK - 1)[None, :] <= q_pos[:, None]
    vis = vis[None, :, None, None, :]
    s = jnp.einsum('btgrd,bngd->btgrn', qg, ck) * HEAD_DIM ** -0.5
    p = jnp.where(vis, jax.nn.softmax(jnp.where(vis, s, NEG), axis=-1), 0.0)
    return jnp.einsum('btgrn,bngd->btgrd', p, cv), p.sum(axis=3)


def _nsa_select_blocks(imp_c, q_pos, L):
    n_cmp = imp_c.shape[-1]
    n_sel = -(-L // SEL_BLOCK)
    c0 = jnp.arange(n_cmp)[:, None] * CMP_STRIDE
    s0 = jnp.arange(n_sel)[None, :] * SEL_BLOCK
    overlap = ((c0 < s0 + SEL_BLOCK) & (c0 + CMP_BLOCK > s0)).astype(jnp.float32)
    imp = jnp.einsum('btgn,nj->btgj', imp_c, overlap)
    j = jnp.arange(n_sel)[None, :]
    cur = q_pos[:, None] // SEL_BLOCK
    forced = ((j == 0) | (j == cur) | (j == cur - 1))[None, :, None, :]
    admissible = (j * SEL_BLOCK <= q_pos[:, None])[None, :, None, :]
    imp = jnp.where(admissible, jnp.where(forced, imp + FORCE_BONUS, imp), NEG)
    return lax.top_k(imp, min(SEL_TOPN, n_sel))[1]


def _nsa_selected(qg, q_pos, sel_idx, ks, vs):
    B, L = ks.shape[:2]
    lp = -(-L // SEL_BLOCK) * SEL_BLOCK
    pad = ((0, 0), (0, lp - L), (0, 0), (0, 0))
    ks_t = jnp.pad(ks, pad).transpose(0, 2, 1, 3)
    vs_t = jnp.pad(vs, pad).transpose(0, 2, 1, 3)
    bi = jnp.arange(B)[:, None, None, None]
    gi = jnp.arange(NSA_KV_HEADS)[None, None, :, None]

    def blk(qb, idxb, posb):
        tok = (idxb[..., None] * SEL_BLOCK + jnp.arange(SEL_BLOCK)).reshape(*idxb.shape[:3], -1)
        kg, vg = ks_t[bi, gi, tok], vs_t[bi, gi, tok]
        mask = (tok <= posb[None, :, None, None])[:, :, :, None, :]
        return _attend_gathered(qb, kg, vg, mask)

    return _over_query_blocks(blk, (qg, sel_idx), q_pos)


def _window_band(qg, kw, vw):
    B, T = kw.shape[:2]
    nb = T // Q_BLOCK
    span = WINDOW + Q_BLOCK
    kp = jnp.pad(kw, ((0, 0), (WINDOW, 0), (0, 0), (0, 0)))
    vp = jnp.pad(vw, ((0, 0), (WINDOW, 0), (0, 0), (0, 0)))
    qb = jnp.moveaxis(qg.reshape(B, nb, Q_BLOCK, *qg.shape[2:]), 1, 0)

    def blk(args):
        qi, i = args
        start = i * Q_BLOCK
        kb = lax.dynamic_slice_in_dim(kp, start, span, axis=1)
        vb = lax.dynamic_slice_in_dim(vp, start, span, axis=1)
        t = start + jnp.arange(Q_BLOCK)
        s = start - WINDOW + jnp.arange(span)
        dd = t[:, None] - s[None, :]
        mask = (dd >= 0) & (dd <= WINDOW) & (s[None, :] >= 0)
        return _attend_shared(qi, kb, vb, mask[None, :, None, None, :])

    out = lax.map(blk, (qb, jnp.arange(nb)))
    return jnp.moveaxis(out, 0, 1).reshape(qg.shape)


def _window_dense(qg, q_pos, kw, vw, k_pos):
    dd = q_pos[:, None] - k_pos[None, :]
    mask = (dd >= 0) & (dd <= WINDOW)
    return _attend_shared(qg, kw, vw, mask[None, :, None, None, :])


def _dsa(qg, q_pos, kd, vd, qi, ki, wi):
    B, L = kd.shape[:2]
    topk = min(DSA_TOPK_MAX, L // 4)
    kd_t, vd_t = kd.transpose(0, 2, 1, 3), vd.transpose(0, 2, 1, 3)
    k_pos = jnp.arange(L)
    bi = jnp.arange(B)[:, None, None, None]
    gi = jnp.arange(DSA_KV_HEADS)[None, None, :, None]

    def blk(qb, qib, wib, posb):
        score = jnp.einsum('bqh,bqhl->bql', wib, jax.nn.relu(jnp.einsum('bqhe,ble->bqhl', qib, ki)))
        score = jnp.where(k_pos[None, None, :] <= posb[None, :, None], score, NEG)
        idx = lax.top_k(score, topk)[1][:, :, None, :]
        kg, vg = kd_t[bi, gi, idx], vd_t[bi, gi, idx]
        mask = (idx <= posb[None, :, None, None])[:, :, :, None, :]
        return _attend_gathered(qb, kg, vg, mask)

    return _over_query_blocks(blk, (qg, qi, wi), q_pos)


def _layer(x, q_pos, lw, past):
    (g_mix, g_mlp, w_in_l, conv_w, a_log, dt_bias, gdn_g, ret_g, cwk, cwv, pek, pev, w_br, w_o, w_up_l, w_down_l) = lw
    conv_buf, s_gdn, s_ret, win_buf, past_rows = past
    f32 = jnp.float32
    B, T, _ = x.shape
    h = (_rmsnorm(x, g_mix) @ w_in_l).astype(f32)
    (qkv_a, z_a, b_a, a_a, q_b, k_b, v_b, g_b, q_c, kc, vc, ks, vs, kw, vw, g_c,
     q_d, k_d, v_d, qi_d, ki_d, wi_d, g_merge) = _split(h, IN_SIZES)

    o_a, conv_new, s_gdn_new = _gdn(qkv_a, z_a, b_a, a_a, conv_buf, s_gdn, conv_w.astype(f32),
                                    a_log.astype(f32), dt_bias.astype(f32), gdn_g)
    o_b, s_ret_new = _retention(q_b, k_b, v_b, g_b, q_pos, s_ret, ret_g.astype(f32))

    new_rows = jnp.concatenate([kc, vc, ks, vs, k_d, v_d, ki_d], axis=-1)
    all_rows = new_rows if past_rows is None else jnp.concatenate([past_rows, new_rows], axis=1)
    L = all_rows.shape[1]
    kc_all, vc_all, ks_all, vs_all, kd_all, vd_all, ki_all = _split(all_rows, PAGED_SIZES)
    heads = lambda a, n: a.reshape(B, L, n, HEAD_DIM)

    qg_c = q_c.reshape(B, T, NSA_KV_HEADS, NSA_GROUP, HEAD_DIM)
    o_cmp, imp = _nsa_compressed(qg_c, q_pos, heads(kc_all, NSA_KV_HEADS), heads(vc_all, NSA_KV_HEADS),
                                 cwk.astype(f32), cwv.astype(f32), pek.astype(f32), pev.astype(f32))
    sel_idx = _nsa_select_blocks(imp, q_pos, L)
    o_sel = _nsa_selected(qg_c, q_pos, sel_idx, heads(ks_all, NSA_KV_HEADS), heads(vs_all, NSA_KV_HEADS))
    win_rows = jnp.concatenate([kw, vw], axis=-1)
    if win_buf is None:
        o_win = _window_band(qg_c, kw.reshape(B, T, NSA_KV_HEADS, HEAD_DIM), vw.reshape(B, T, NSA_KV_HEADS, HEAD_DIM))
        win_new = win_rows[:, T - min(WINDOW, T):]
    else:
        wb = win_buf.shape[1]
        win_all = jnp.concatenate([win_buf, win_rows], axis=1)
        kw_all, vw_all = jnp.split(win_all, 2, axis=-1)
        k_pos = q_pos[0] - wb + jnp.arange(wb + T)
        o_win = _window_dense(qg_c, q_pos, kw_all.reshape(B, wb + T, NSA_KV_HEADS, HEAD_DIM),
                              vw_all.reshape(B, wb + T, NSA_KV_HEADS, HEAD_DIM), k_pos)
        win_new = win_all[:, T:]
    gts = jax.nn.sigmoid(g_c.reshape(B, T, 3, NSA_KV_HEADS, NSA_GROUP))[..., None]
    o_c = (gts[:, :, 0] * o_cmp + gts[:, :, 1] * o_sel + gts[:, :, 2] * o_win).reshape(B, T, MIX_WIDTH)

    o_d = _dsa(q_d.reshape(B, T, DSA_KV_HEADS, DSA_GROUP, HEAD_DIM), q_pos, heads(kd_all, DSA_KV_HEADS),
               heads(vd_all, DSA_KV_HEADS), qi_d.reshape(B, T, IDX_HEADS, IDX_DIM), ki_all, wi_d).reshape(B, T, MIX_WIDTH)

    gm = jax.nn.sigmoid(g_merge.reshape(B, T, N_BRANCH, D_MODEL))
    mixed = sum(gm[:, :, i] * (o.astype(x.dtype) @ w_br[i]).astype(f32) for i, o in enumerate((o_a, o_b, o_c, o_d)))
    x = x + mixed.astype(x.dtype) @ w_o
    x = x + jnp.square(jax.nn.relu(_rmsnorm(x, g_mlp) @ w_up_l)) @ w_down_l
    return x, (new_rows, conv_new, s_gdn_new, s_ret_new, win_new)


def setup_inputs(seed: int = 0) -> dict:
    key = jax.random.key(seed)
    ks = jax.random.split(key, 25)
    f32 = jnp.float32
    nrm = lambda k, shape, scale: jax.random.normal(k, shape, f32) * scale
    n_pages = PAST_LEN // PAGE_SIZE
    n_used = DEC_BATCH * n_pages
    n_pool = n_used + max(1, n_used // 4)
    win_len = min(WINDOW, PAST_LEN)
    page_table = jax.random.permutation(ks[7], n_pool)[:n_used].reshape(DEC_BATCH, n_pages).astype(jnp.int32)
    return {
        'x_prompt': nrm(ks[0], (BATCH, SEQ, D_MODEL), 1.0),
        'x_sample': nrm(ks[1], (DEC_BATCH, DEC_SEQ, D_MODEL), 1.0),
        'cache_kv': nrm(ks[2], (n_pool, PAGE_SIZE, DEPTH, PAGED_W), 1.0),
        'cache_nsa_window': nrm(ks[3], (DEC_BATCH, DEPTH, win_len, WIN_W), 1.0),
        'state_gdn': nrm(ks[4], (DEC_BATCH, DEPTH, N_HEADS, HEAD_DIM, HEAD_DIM), 0.1),
        'state_gdn_conv': nrm(ks[5], (DEC_BATCH, DEPTH, GDN_CONV - 1, 3 * MIX_WIDTH), 1.0),
        'state_retention': nrm(ks[6], (DEC_BATCH, DEPTH, N_HEADS, HEAD_DIM, HEAD_DIM), 0.5),
        'page_table': page_table,
        'norm_mix': 1.0 + nrm(ks[8], (DEPTH, D_MODEL), 0.01),
        'norm_mlp': 1.0 + nrm(ks[9], (DEPTH, D_MODEL), 0.01),
        'norm_final': 1.0 + nrm(ks[10], (D_MODEL,), 0.01),
        'w_in': nrm(ks[11], (DEPTH, D_MODEL, IN_WIDTH), D_MODEL ** -0.5),
        'gdn_conv_w': nrm(ks[12], (DEPTH, GDN_CONV, 3 * MIX_WIDTH), GDN_CONV ** -0.5),
        'gdn_a_log': jnp.log(jax.random.uniform(ks[13], (DEPTH, N_HEADS), f32, 1.0, 16.0)),
        'gdn_dt_bias': jax.random.uniform(ks[14], (DEPTH, N_HEADS), f32, -4.0, -2.0),
        'gdn_norm': 1.0 + nrm(ks[15], (DEPTH, HEAD_DIM), 0.01),
        'ret_norm': 1.0 + nrm(ks[16], (DEPTH, MIX_WIDTH), 0.01),
        'nsa_cmp_wk': nrm(ks[17], (DEPTH, CMP_BLOCK, HEAD_DIM, HEAD_DIM), (CMP_BLOCK * HEAD_DIM) ** -0.5),
        'nsa_cmp_wv': nrm(ks[18], (DEPTH, CMP_BLOCK, HEAD_DIM, HEAD_DIM), (CMP_BLOCK * HEAD_DIM) ** -0.5),
        'nsa_cmp_pe_k': nrm(ks[19], (DEPTH, CMP_BLOCK, HEAD_DIM), 0.1),
        'nsa_cmp_pe_v': nrm(ks[20], (DEPTH, CMP_BLOCK, HEAD_DIM), 0.1),
        'w_branch': nrm(ks[21], (DEPTH, N_BRANCH, MIX_WIDTH, D_MODEL), MIX_WIDTH ** -0.5),
        'w_out': nrm(ks[22], (DEPTH, D_MODEL, D_MODEL), D_MODEL ** -0.5),
        'w_up': nrm(ks[23], (DEPTH, D_MODEL, D_FF), D_MODEL ** -0.5),
        'w_down': nrm(ks[24], (DEPTH, D_FF, D_MODEL), D_FF ** -0.5),
    }


def reference(x_prompt, x_sample, cache_kv, cache_nsa_window, state_gdn, state_gdn_conv, state_retention,
              page_table, norm_mix, norm_mlp, norm_final, w_in, gdn_conv_w, gdn_a_log, gdn_dt_bias, gdn_norm,
              ret_norm, nsa_cmp_wk, nsa_cmp_wv, nsa_cmp_pe_k, nsa_cmp_pe_v, w_branch, w_out, w_up, w_down):
    f32 = jnp.float32
    bp, tp, _ = x_prompt.shape
    bs, ts, _ = x_sample.shape
    past_len = page_table.shape[1] * cache_kv.shape[1]
    pos_p = jnp.arange(tp, dtype=jnp.int32)
    pos_s = past_len + jnp.arange(ts, dtype=jnp.int32)
    zero_conv = jnp.zeros((bp, GDN_CONV - 1, 3 * MIX_WIDTH), f32)
    zero_state = jnp.zeros((bp, N_HEADS, HEAD_DIM, HEAD_DIM), f32)
    hp, hs = x_prompt, x_sample
    out_p, out_s = [], []
    for l in range(DEPTH):
        lw = (norm_mix[l], norm_mlp[l], w_in[l], gdn_conv_w[l], gdn_a_log[l], gdn_dt_bias[l], gdn_norm[l],
              ret_norm[l], nsa_cmp_wk[l], nsa_cmp_wv[l], nsa_cmp_pe_k[l], nsa_cmp_pe_v[l], w_branch[l],
              w_out[l], w_up[l], w_down[l])
        hp, st_p = _layer(hp, pos_p, lw, (zero_conv, zero_state, zero_state, None, None))
        out_p.append(st_p)
        past_rows = cache_kv[page_table, :, l].reshape(bs, past_len, PAGED_W).astype(f32)
        past_s = (state_gdn_conv[:, l].astype(f32), state_gdn[:, l].astype(f32), state_retention[:, l].astype(f32),
                  cache_nsa_window[:, l].astype(f32), past_rows)
        hs, st_s = _layer(hs, pos_s, lw, past_s)
        out_s.append(st_s)
    y_prompt = _rmsnorm(hp, norm_final)
    y_sample = _rmsnorm(hs, norm_final)
    stack = lambda outs, j, axis, like: jnp.stack([o[j] for o in outs], axis=axis).astype(like.dtype)
    new_kv_prompt = stack(out_p, 0, 2, cache_kv)
    new_kv_sample = stack(out_s, 0, 2, cache_kv)
    gdn_conv_prompt = stack(out_p, 1, 1, state_gdn_conv)
    gdn_conv_sample = stack(out_s, 1, 1, state_gdn_conv)
    gdn_state_prompt = stack(out_p, 2, 1, state_gdn)
    gdn_state_sample = stack(out_s, 2, 1, state_gdn)
    ret_state_prompt = stack(out_p, 3, 1, state_retention)
    ret_state_sample = stack(out_s, 3, 1, state_retention)
    win_prompt = stack(out_p, 4, 1, cache_nsa_window)
    win_sample = stack(out_s, 4, 1, cache_nsa_window)
    return (y_prompt, y_sample, new_kv_prompt, new_kv_sample, gdn_state_prompt, gdn_state_sample,
            gdn_conv_prompt, gdn_conv_sample, ret_state_prompt, ret_state_sample, win_prompt, win_sample)
```

```python
import functools
import math

import jax
import jax.numpy as jnp
import numpy as np
from jax import lax
from jax.experimental import pallas as pl
from jax.experimental.pallas import tpu as pltpu

HEAD_DIM = 128
N_BRANCH = 4
EPS = 1e-6
NEG = -1e30
GDN_CONV = 4
GDN_CHUNK = 64
RET_CHUNK = 64
ROPE_BASE = 10000.0
KV_HEADS = 2
CMP_BLOCK = 32
CMP_STRIDE = 16
SEL_BLOCK = 64
SEL_TOPN = 16
WINDOW = 512
FORCE_BONUS = 1e4
IDX_HEADS = 16
IDX_DIM = 64
DSA_TOPK_MAX = 256
KVW = KV_HEADS * HEAD_DIM
PAGED_SIZES = (KVW, KVW, KVW, KVW, KVW, KVW, IDX_DIM)
PAGED_W = sum(PAGED_SIZES)

LANE = 128
VMEM_LIMIT = 56 * 1024 * 1024

F32 = jnp.float32
BF16 = jnp.bfloat16


def _round_up(n, m):
    return -(-n // m) * m


def _pick_tile(dim, target, align):
    best = None
    for t in range(align, min(dim, target) + 1, align):
        if dim % t == 0:
            best = t
    return best if best is not None else dim


class _Layout:
    def __init__(self, d_model):
        mix = d_model // N_BRANCH
        nh = mix // HEAD_DIM
        self.d_model, self.mix, self.nh = d_model, mix, nh
        self.in_sizes = (
            3 * mix, mix, nh, nh,
            mix, mix, mix, mix,
            mix, KVW, KVW, KVW, KVW, KVW, KVW, 3 * nh,
            mix, KVW, KVW, IDX_HEADS * IDX_DIM, IDX_DIM, IDX_HEADS,
            N_BRANCH * d_model)
        self.order = [0, 1, 4, 5, 6, 7, 8, 16, 19, 22, 9, 10, 11, 12, 17, 18, 20, None, 13, 14, 2, 3, 15, 21, None]
        self.off = {}
        pos = 0
        self.pads = []
        for idx in self.order:
            if idx is None:
                pad = _round_up(pos, LANE) - pos
                self.pads.append(pad)
                pos += pad
            else:
                self.off[idx] = pos
                pos += self.in_sizes[idx]
        self.width = pos


def _pack_w_in(w_in, lay):
    starts = np.concatenate([[0], np.cumsum(lay.in_sizes)]).tolist()
    parts, pads = [], iter(lay.pads)
    for idx in lay.order:
        if idx is None:
            parts.append(jnp.zeros(w_in.shape[:2] + (next(pads),), BF16))
        else:
            parts.append(w_in[:, :, starts[idx]:starts[idx + 1]].astype(BF16))
    return jnp.concatenate(parts, axis=-1)


def _rmsnorm_body(x_ref, g_ref, o_ref):
    x = x_ref[...]
    y = x * lax.rsqrt(jnp.mean(x * x, axis=-1, keepdims=True) + EPS)
    o_ref[...] = (y * g_ref[...]).astype(o_ref.dtype)


def _rmsnorm(x, g, out_dtype):
    m, d = x.shape
    tm = _pick_tile(m, 256, 8)
    return pl.pallas_call(
        _rmsnorm_body,
        grid=(m // tm,),
        in_specs=[pl.BlockSpec((tm, d), lambda i: (i, 0)), pl.BlockSpec((1, d), lambda i: (0, 0))],
        out_specs=pl.BlockSpec((tm, d), lambda i: (i, 0)),
        out_shape=jax.ShapeDtypeStruct((m, d), out_dtype),
        compiler_params=pltpu.CompilerParams(dimension_semantics=("parallel",), vmem_limit_bytes=VMEM_LIMIT),
        name="rmsnorm",
    )(x, g.reshape(1, d).astype(F32))


def _mm_body(*refs, nk, epilogue):
    if epilogue == "residual":
        x_ref, w_ref, r_ref, o_ref, acc_ref = refs
    else:
        x_ref, w_ref, o_ref, acc_ref = refs
    k = pl.program_id(2)

    @pl.when(k == 0)
    def _():
        acc_ref[...] = jnp.zeros_like(acc_ref)

    acc_ref[...] += jnp.dot(x_ref[...], w_ref[...], preferred_element_type=F32)

    @pl.when(k == nk - 1)
    def _():
        a = acc_ref[...]
        if epilogue == "relu2":
            a = jnp.square(jnp.maximum(a, 0.0))
        elif epilogue == "residual":
            a = a + r_ref[...]
        o_ref[...] = a.astype(o_ref.dtype)


def _matmul(x, w, *, epilogue="none", res=None, out_dtype=F32, name="matmul"):
    m, kdim = x.shape
    n = w.shape[1]
    tm = _pick_tile(m, 1024, 8)
    tn = _pick_tile(n, 2048, LANE)
    tk = _pick_tile(kdim, 1024, LANE)
    nk = kdim // tk
    in_specs = [pl.BlockSpec((tm, tk), lambda i, j, k: (i, k)), pl.BlockSpec((tk, tn), lambda i, j, k: (k, j))]
    args = [x, w]
    if epilogue == "residual":
        in_specs.append(pl.BlockSpec((tm, tn), lambda i, j, k: (i, j)))
        args.append(res)
    return pl.pallas_call(
        functools.partial(_mm_body, nk=nk, epilogue=epilogue),
        grid=(m // tm, n // tn, nk),
        in_specs=in_specs,
        out_specs=pl.BlockSpec((tm, tn), lambda i, j, k: (i, j)),
        out_shape=jax.ShapeDtypeStruct((m, n), out_dtype),
        scratch_shapes=[pltpu.VMEM((tm, tn), F32)],
        compiler_params=pltpu.CompilerParams(
            dimension_semantics=("parallel", "parallel", "arbitrary"), vmem_limit_bytes=VMEM_LIMIT),
        name=name,
    )(*args)


def _merge_body(o_ref, w_ref, g_ref, out_ref, acc_ref):
    b = pl.program_id(2)

    @pl.when(b == 0)
    def _():
        acc_ref[...] = jnp.zeros_like(acc_ref)

    acc_ref[...] += jax.nn.sigmoid(g_ref[...]) * jnp.dot(o_ref[0], w_ref[0], preferred_element_type=F32)

    @pl.when(b == N_BRANCH - 1)
    def _():
        out_ref[...] = acc_ref[...].astype(out_ref.dtype)


def _merge_branches(o_stack, w_br, h, lay):
    _, m, mix = o_stack.shape
    d = lay.d_model
    tm = _pick_tile(m, 1024, 8)
    tn = _pick_tile(math.gcd(d, lay.off[22]), 1024, LANE)
    g0, gstep = lay.off[22] // tn, d // tn
    return pl.pallas_call(
        _merge_body,
        grid=(m // tm, d // tn, N_BRANCH),
        in_specs=[pl.BlockSpec((1, tm, mix), lambda i, j, b: (b, i, 0)),
                  pl.BlockSpec((1, mix, tn), lambda i, j, b: (b, 0, j)),
                  pl.BlockSpec((tm, tn), lambda i, j, b: (i, g0 + b * gstep + j))],
        out_specs=pl.BlockSpec((tm, tn), lambda i, j, b: (i, j)),
        out_shape=jax.ShapeDtypeStruct((m, d), BF16),
        scratch_shapes=[pltpu.VMEM((tm, tn), F32)],
        compiler_params=pltpu.CompilerParams(
            dimension_semantics=("parallel", "parallel", "arbitrary"), vmem_limit_bytes=VMEM_LIMIT),
        name="merge_branches",
    )(o_stack, w_br, h)


def _l2norm(x):
    return x * lax.rsqrt(jnp.sum(x * x, axis=-1, keepdims=True) + EPS)


def _rms(x, g):
    y = x * lax.rsqrt(jnp.mean(x * x, axis=-1, keepdims=True) + EPS)
    return y * g


def _chunk_len(t, c):
    return c if t % c == 0 else t


def _rope(x, pos):
    half = x.shape[-1] // 2
    freq = jnp.exp(-math.log(ROPE_BASE) * jnp.arange(half, dtype=F32) / half)
    ang = pos.astype(F32)[:, None] * freq
    cos, sin = jnp.cos(ang)[None, :, None, :], jnp.sin(ang)[None, :, None, :]
    x1, x2 = x[..., :half], x[..., half:]
    return jnp.concatenate([x1 * cos - x2 * sin, x1 * sin + x2 * cos], axis=-1)


def _gated_delta_rule(q, k, v, g, beta, s0):
    b, t, h, _ = q.shape
    dv = v.shape[-1]
    c = _chunk_len(t, GDN_CHUNK)
    n = t // c
    ch = lambda a: jnp.swapaxes(a.reshape(b, n, c, h, *a.shape[3:]), 2, 3)
    q, k, v, g, beta = ch(q), ch(k), ch(v), ch(g), ch(beta)
    gc = jnp.cumsum(g, axis=-1)
    incl = jnp.tril(jnp.ones((c, c), bool))
    strict = jnp.tril(jnp.ones((c, c), F32), -1)
    decay = jnp.exp(jnp.where(incl, gc[..., :, None] - gc[..., None, :], -jnp.inf))
    kb = k * beta[..., None]
    a_mat = jnp.einsum('bnhid,bnhjd->bnhij', kb, k) * decay * strict
    eye = jnp.eye(c, dtype=F32)
    t_mat = lax.linalg.triangular_solve(a_mat + eye, jnp.broadcast_to(eye, a_mat.shape), left_side=True, lower=True)
    u = jnp.einsum('bnhij,bnhjd->bnhid', t_mat, v * beta[..., None])
    w = jnp.einsum('bnhij,bnhjd->bnhid', t_mat, kb * jnp.exp(gc)[..., None])
    qk = jnp.einsum('bnhid,bnhjd->bnhij', q, k) * decay
    qg = q * jnp.exp(gc)[..., None]
    kd = k * jnp.exp(gc[..., -1:] - gc)[..., None]
    glast = jnp.exp(gc[..., -1])

    def step(s, xs):
        u_i, w_i, qk_i, qg_i, kd_i, gl_i = xs
        v_new = u_i - jnp.einsum('bhcd,bhde->bhce', w_i, s)
        o = jnp.einsum('bhcd,bhde->bhce', qg_i, s) + jnp.einsum('bhij,bhje->bhie', qk_i, v_new)
        s = s * gl_i[..., None, None] + jnp.einsum('bhcd,bhce->bhde', kd_i, v_new)
        return s, o

    xs = tuple(jnp.moveaxis(a, 1, 0) for a in (u, w, qk, qg, kd, glast))
    s, o = lax.scan(step, s0, xs)
    return o.transpose(1, 0, 3, 2, 4).reshape(b, t, h, dv), s


def _gdn(qkv, z, b_raw, a_raw, conv_buf, s0, conv_w, a_log, dt_bias, norm_g, nh):
    b, t, _ = qkv.shape
    xc = jnp.concatenate([conv_buf, qkv], axis=1)
    y = jax.nn.silu(sum(xc[:, j:j + t] * conv_w[j] for j in range(GDN_CONV)))
    q, k, v = [a.reshape(b, t, nh, HEAD_DIM) for a in jnp.split(y, 3, axis=-1)]
    q = _l2norm(q) * HEAD_DIM ** -0.5
    k = _l2norm(k)
    beta = jax.nn.sigmoid(b_raw)
    g = -jnp.exp(a_log) * jax.nn.softplus(a_raw + dt_bias)
    o, s = _gated_delta_rule(q, k, v, g, beta, s0)
    o = _rms(o, norm_g) * jax.nn.silu(z.reshape(b, t, nh, HEAD_DIM))
    return o.reshape(b, t, nh * HEAD_DIM), xc[:, t:], s


def _retention(q, k, v, gate, pos, s0, gn_g, nh):
    b, t, _ = q.shape
    h, d = nh, HEAD_DIM
    q = _rope(q.reshape(b, t, h, d), pos)
    k = _rope(k.reshape(b, t, h, d), pos) * d ** -0.5
    v = v.reshape(b, t, h, d)
    lg = jnp.log1p(-jnp.exp2(-5.0 - jnp.arange(h, dtype=F32)))
    c = _chunk_len(t, RET_CHUNK)
    n = t // c
    qc, kc, vc = (a.reshape(b, n, c, h, d) for a in (q, k, v))
    i = jnp.arange(c, dtype=F32)
    diff = i[:, None] - i[None, :]
    dmat = jnp.where(diff >= 0, jnp.exp(jnp.maximum(diff, 0.0)[None] * lg[:, None, None]), 0.0)
    o_in = jnp.einsum('bnhij,bnjhe->bnihe', jnp.einsum('bnihd,bnjhd->bnhij', qc, kc) * dmat, vc)
    q_dec = qc * jnp.exp((i + 1.0)[:, None] * lg)[:, :, None]
    k_dec = kc * jnp.exp((c - 1.0 - i)[:, None] * lg)[:, :, None]
    kv = jnp.einsum('bnjhd,bnjhe->bnhde', k_dec, vc)
    chunk_decay = jnp.exp(c * lg)[:, None, None]

    def step(s, xs):
        qd_i, kv_i = xs
        o = jnp.einsum('bihd,bhde->bihe', qd_i, s)
        return s * chunk_decay + kv_i, o

    s, o_x = lax.scan(step, s0, (jnp.moveaxis(q_dec, 1, 0), jnp.moveaxis(kv, 1, 0)))
    o = (o_in + jnp.moveaxis(o_x, 0, 1)).reshape(b, t, h, d)
    oc = o - jnp.mean(o, axis=-1, keepdims=True)
    o = oc * lax.rsqrt(jnp.mean(oc * oc, axis=-1, keepdims=True) + EPS)
    return jax.nn.silu(gate) * (o.reshape(b, t, h * d) * gn_g), s


def _compress(k, w, pe):
    b, l, g, d = k.shape
    n_cmp = (l - CMP_BLOCK) // CMP_STRIDE + 1
    x = k[:, :CMP_STRIDE * (n_cmp + 1)].reshape(b, n_cmp + 1, CMP_STRIDE, g, d)
    lo = jnp.einsum('bncgd,cde->bnge', x, w[:CMP_STRIDE])
    hi = jnp.einsum('bncgd,cde->bnge', x, w[CMP_STRIDE:])
    bias = jnp.einsum('cd,cde->e', pe, w, precision=lax.Precision.HIGHEST)
    return lo[:, :-1] + hi[:, 1:] + bias


def _masked_attend(qg, k, v, mask):
    s = jnp.einsum('btgrd,blgd->btgrl', qg, k) * HEAD_DIM ** -0.5
    p = jax.nn.softmax(jnp.where(mask, s, NEG), axis=-1)
    return jnp.einsum('btgrl,blgd->btgrd', p, v)


def _nsa(q_c, g_c, q_pos, kc_all, vc_all, ks_all, vs_all, kw_all, vw_all, k_pos_win, cwk, cwv, pek, pev, nh):
    b, t, _ = q_c.shape
    l = kc_all.shape[1]
    grp = nh // KV_HEADS
    qg = q_c.reshape(b, t, KV_HEADS, grp, HEAD_DIM)
    heads = lambda a: a.reshape(b, a.shape[1], KV_HEADS, HEAD_DIM)
    ck, cv = _compress(heads(kc_all), cwk, pek), _compress(heads(vc_all), cwv, pev)
    n_cmp = ck.shape[1]
    vis = ((jnp.arange(n_cmp) * CMP_STRIDE + CMP_BLOCK - 1)[None, :] <= q_pos[:, None])[None, :, None, None, :]
    s = jnp.einsum('btgrd,bngd->btgrn', qg, ck) * HEAD_DIM ** -0.5
    p = jnp.where(vis, jax.nn.softmax(jnp.where(vis, s, NEG), axis=-1), 0.0)
    o_cmp = jnp.einsum('btgrn,bngd->btgrd', p, cv)
    imp_c = p.sum(axis=3)
    n_sel = -(-l // SEL_BLOCK)
    c0 = jnp.arange(n_cmp)[:, None] * CMP_STRIDE
    s0 = jnp.arange(n_sel)[None, :] * SEL_BLOCK
    overlap = ((c0 < s0 + SEL_BLOCK) & (c0 + CMP_BLOCK > s0)).astype(F32)
    imp = jnp.einsum('btgn,nj->btgj', imp_c, overlap, precision=lax.Precision.HIGHEST)
    j = jnp.arange(n_sel)[None, :]
    cur = q_pos[:, None] // SEL_BLOCK
    forced = ((j == 0) | (j == cur) | (j == cur - 1))[None, :, None, :]
    admissible = (j * SEL_BLOCK <= q_pos[:, None])[None, :, None, :]
    imp = jnp.where(admissible, jnp.where(forced, imp + FORCE_BONUS, imp), NEG)
    a_i, a_j = imp[..., :, None], imp[..., None, :]
    idx = jnp.arange(n_sel)
    ahead = (a_i > a_j) | ((a_i == a_j) & (idx[:, None] < idx[None, :]))
    rank = ahead.sum(axis=-2)
    selected = rank < min(SEL_TOPN, n_sel)
    tok = jnp.arange(l)
    sel_mask = jnp.take(selected, tok // SEL_BLOCK, axis=-1) & (tok[None, :] <= q_pos[:, None])[None, :, None, :]
    o_sel = _masked_attend(qg, heads(ks_all), heads(vs_all), sel_mask[:, :, :, None, :])
    dd = q_pos[:, None] - k_pos_win[None, :]
    win_mask = ((dd >= 0) & (dd <= WINDOW) & (k_pos_win[None, :] >= 0))[None, :, None, None, :]
    o_win = _masked_attend(qg, heads(kw_all), heads(vw_all), win_mask)
    gts = jax.nn.sigmoid(g_c.reshape(b, t, 3, KV_HEADS, grp))[..., None]
    return (gts[:, :, 0] * o_cmp + gts[:, :, 1] * o_sel + gts[:, :, 2] * o_win).reshape(b, t, nh * HEAD_DIM)


def _dsa(q_d, qi, wi, q_pos, kd_all, vd_all, ki_all, nh):
    b, t, _ = q_d.shape
    l = kd_all.shape[1]
    grp = nh // KV_HEADS
    topk = min(DSA_TOPK_MAX, l // 4)
    qg = q_d.reshape(b, t, KV_HEADS, grp, HEAD_DIM)
    heads = lambda a: a.reshape(b, l, KV_HEADS, HEAD_DIM)
    dots = jnp.einsum('bqhe,ble->bqhl', qi.reshape(b, t, IDX_HEADS, IDX_DIM), ki_all)
    score = jnp.sum(wi[..., None] * jax.nn.relu(dots), axis=2)
    causal = (jnp.arange(l)[None, :] <= q_pos[:, None])[None]
    score = jnp.where(causal, score, NEG)
    thr = lax.top_k(score, topk)[0][..., -1:]
    gt = score > thr
    eq = score == thr
    need = topk - gt.sum(axis=-1, keepdims=True)
    sel = gt | (eq & (jnp.cumsum(eq.astype(jnp.int32), axis=-1) <= need))
    mask = (sel & causal)[:, :, None, None, :]
    return _masked_attend(qg, heads(kd_all), heads(vd_all), mask).reshape(b, t, nh * HEAD_DIM)


def _layer(x, q_pos, lw, past, lay):
    (g_mix, g_mlp, w_in_p, conv_w, a_log, dt_bias, gdn_g, ret_g, cwk, cwv, pek, pev, w_br, w_o, w_up_l, w_down_l) = lw
    conv_buf, s_gdn, s_ret, win_buf, past_rows = past
    b, t, d = x.shape
    m = b * t
    mix, nh = lay.mix, lay.nh
    x2d = x.reshape(m, d)
    h = _matmul(_rmsnorm(x2d, g_mix, BF16), w_in_p, name="in_proj")
    h3 = h.reshape(b, t, lay.width)
    seg = lambda idx: h3[:, :, lay.off[idx]:lay.off[idx] + lay.in_sizes[idx]]

    o_a, conv_new, s_gdn_new = _gdn(seg(0), seg(1), seg(2), seg(3), conv_buf, s_gdn, conv_w, a_log, dt_bias, gdn_g, nh)
    o_b, s_ret_new = _retention(seg(4), seg(5), seg(6), seg(7), q_pos, s_ret, ret_g, nh)

    new_rows = h3[:, :, lay.off[9]:lay.off[9] + PAGED_W]
    all_rows = new_rows if past_rows is None else jnp.concatenate([past_rows, new_rows], axis=1)
    kc_all, vc_all, ks_all, vs_all, kd_all, vd_all, ki_all = jnp.split(
        all_rows, np.cumsum(PAGED_SIZES)[:-1].tolist(), axis=-1)
    win_rows = h3[:, :, lay.off[13]:lay.off[13] + 2 * KVW]
    if win_buf is None:
        win_all = win_rows
        k_pos_win = q_pos
        win_new = win_rows[:, t - min(WINDOW, t):]
    else:
        wb = win_buf.shape[1]
        win_all = jnp.concatenate([win_buf, win_rows], axis=1)
        k_pos_win = q_pos[0] - wb + jnp.arange(wb + t)
        win_new = win_all[:, t:]
    kw_all, vw_all = jnp.split(win_all, 2, axis=-1)
    o_c = _nsa(seg(8), seg(15), q_pos, kc_all, vc_all, ks_all, vs_all, kw_all, vw_all, k_pos_win,
               cwk, cwv, pek, pev, nh)
    o_d = _dsa(seg(16), seg(19), seg(21), q_pos, kd_all, vd_all, ki_all, nh)

    o_stack = jnp.stack([o.reshape(m, mix) for o in (o_a, o_b, o_c, o_d)]).astype(BF16)
    mixed = _merge_branches(o_stack, w_br, h, lay)
    x2 = _matmul(mixed, w_o, epilogue="residual", res=x2d, name="out_proj")
    up = _matmul(_rmsnorm(x2, g_mlp, BF16), w_up_l, epilogue="relu2", out_dtype=BF16, name="ffn_up")
    x3 = _matmul(up, w_down_l, epilogue="residual", res=x2, name="ffn_down")
    return x3.reshape(b, t, d), (new_rows, conv_new, s_gdn_new, s_ret_new, win_new)


def kernel(x_prompt, x_sample, cache_kv, cache_nsa_window, state_gdn, state_gdn_conv, state_retention, page_table, norm_mix, norm_mlp, norm_final, w_in, gdn_conv_w, gdn_a_log, gdn_dt_bias, gdn_norm, ret_norm, nsa_cmp_wk, nsa_cmp_wv, nsa_cmp_pe_k, nsa_cmp_pe_v, w_branch, w_out, w_up, w_down):
    bp, tp, d = x_prompt.shape
    bs, ts, _ = x_sample.shape
    depth = w_in.shape[0]
    lay = _Layout(d)
    past_len = page_table.shape[1] * cache_kv.shape[1]
    pos_p = jnp.arange(tp, dtype=jnp.int32)
    pos_s = past_len + jnp.arange(ts, dtype=jnp.int32)
    zero_conv = jnp.zeros((bp, GDN_CONV - 1, 3 * lay.mix), F32)
    zero_state = jnp.zeros((bp, lay.nh, HEAD_DIM, HEAD_DIM), F32)
    w_in_p = _pack_w_in(w_in, lay)
    w_br_b, w_o_b, w_up_b, w_down_b = (a.astype(BF16) for a in (w_branch, w_out, w_up, w_down))
    hp, hs = x_prompt, x_sample
    out_p, out_s = [], []
    for l in range(depth):
        lw = (norm_mix[l], norm_mlp[l], w_in_p[l], gdn_conv_w[l], gdn_a_log[l], gdn_dt_bias[l], gdn_norm[l],
              ret_norm[l], nsa_cmp_wk[l], nsa_cmp_wv[l], nsa_cmp_pe_k[l], nsa_cmp_pe_v[l], w_br_b[l],
              w_o_b[l], w_up_b[l], w_down_b[l])
        hp, st_p = _layer(hp, pos_p, lw, (zero_conv, zero_state, zero_state, None, None), lay)
        out_p.append(st_p)
        past_rows = cache_kv[page_table, :, l].reshape(bs, past_len, PAGED_W)
        past_s = (state_gdn_conv[:, l], state_gdn[:, l], state_retention[:, l], cache_nsa_window[:, l], past_rows)
        hs, st_s = _layer(hs, pos_s, lw, past_s, lay)
        out_s.append(st_s)
    y_prompt = _rmsnorm(hp.reshape(bp * tp, d), norm_final, F32).reshape(bp, tp, d)
    y_sample = _rmsnorm(hs.reshape(bs * ts, d), norm_final, F32).reshape(bs, ts, d)
    stack = lambda outs, j, axis: jnp.stack([o[j] for o in outs], axis=axis)
    return (y_prompt, y_sample,
            stack(out_p, 0, 2), stack(out_s, 0, 2),
            stack(out_p, 2, 1), stack(out_s, 2, 1),
            stack(out_p, 1, 1), stack(out_s, 1, 1),
            stack(out_p, 3, 1), stack(out_s, 3, 1),
            stack(out_p, 4, 1), stack(out_s, 4, 1))
```

```python
import functools
import math

import jax
import jax.numpy as jnp
import numpy as np
from jax import lax
from jax.experimental import pallas as pl
from jax.experimental.pallas import tpu as pltpu

HEAD_DIM = 128
N_BRANCH = 4
EPS = 1e-6
NEG = -1e30
GDN_CONV = 4
GDN_CHUNK = 64
RET_CHUNK = 64
ROPE_BASE = 10000.0
KV_HEADS = 2
CMP_BLOCK = 32
CMP_STRIDE = 16
SEL_BLOCK = 64
SEL_TOPN = 16
WINDOW = 512
FORCE_BONUS = 1e4
IDX_HEADS = 16
IDX_DIM = 64
DSA_TOPK_MAX = 256
Q_BLOCK = 128
KVW = KV_HEADS * HEAD_DIM
PAGED_SIZES = (KVW, KVW, KVW, KVW, KVW, KVW, IDX_DIM)
PAGED_W = sum(PAGED_SIZES)

LANE = 128
VMEM_LIMIT = 56 * 1024 * 1024

F32 = jnp.float32
BF16 = jnp.bfloat16


def _round_up(n, m):
    return -(-n // m) * m


def _pick_tile(dim, target, align):
    best = None
    for t in range(align, min(dim, target) + 1, align):
        if dim % t == 0:
            best = t
    return best if best is not None else dim


class _Layout:
    def __init__(self, d_model):
        mix = d_model // N_BRANCH
        nh = mix // HEAD_DIM
        self.d_model, self.mix, self.nh = d_model, mix, nh
        self.in_sizes = (
            3 * mix, mix, nh, nh,
            mix, mix, mix, mix,
            mix, KVW, KVW, KVW, KVW, KVW, KVW, 3 * nh,
            mix, KVW, KVW, IDX_HEADS * IDX_DIM, IDX_DIM, IDX_HEADS,
            N_BRANCH * d_model)
        self.order = [0, 1, 4, 5, 6, 7, 8, 16, 19, 22, 9, 10, 11, 12, 17, 18, 20, None, 13, 14, 2, 3, 15, 21, None]
        self.off = {}
        pos = 0
        self.pads = []
        for idx in self.order:
            if idx is None:
                pad = _round_up(pos, LANE) - pos
                self.pads.append(pad)
                pos += pad
            else:
                self.off[idx] = pos
                pos += self.in_sizes[idx]
        self.width = pos


def _pack_w_in(w_in, lay):
    starts = np.concatenate([[0], np.cumsum(lay.in_sizes)]).tolist()
    parts, pads = [], iter(lay.pads)
    for idx in lay.order:
        if idx is None:
            parts.append(jnp.zeros(w_in.shape[:2] + (next(pads),), BF16))
        else:
            parts.append(w_in[:, :, starts[idx]:starts[idx + 1]].astype(BF16))
    return jnp.concatenate(parts, axis=-1)


def _rmsnorm_body(x_ref, g_ref, o_ref):
    x = x_ref[...]
    y = x * lax.rsqrt(jnp.mean(x * x, axis=-1, keepdims=True) + EPS)
    o_ref[...] = (y * g_ref[...]).astype(o_ref.dtype)


def _rmsnorm(x, g, out_dtype):
    m, d = x.shape
    tm = _pick_tile(m, 256, 8)
    return pl.pallas_call(
        _rmsnorm_body,
        grid=(m // tm,),
        in_specs=[pl.BlockSpec((tm, d), lambda i: (i, 0)), pl.BlockSpec((1, d), lambda i: (0, 0))],
        out_specs=pl.BlockSpec((tm, d), lambda i: (i, 0)),
        out_shape=jax.ShapeDtypeStruct((m, d), out_dtype),
        compiler_params=pltpu.CompilerParams(dimension_semantics=("parallel",), vmem_limit_bytes=VMEM_LIMIT),
        name="rmsnorm",
    )(x, g.reshape(1, d).astype(F32))


def _mm_body(*refs, nk, epilogue):
    if epilogue == "residual":
        x_ref, w_ref, r_ref, o_ref, acc_ref = refs
    else:
        x_ref, w_ref, o_ref, acc_ref = refs
    k = pl.program_id(2)

    @pl.when(k == 0)
    def _():
        acc_ref[...] = jnp.zeros_like(acc_ref)

    acc_ref[...] += jnp.dot(x_ref[...], w_ref[...], preferred_element_type=F32)

    @pl.when(k == nk - 1)
    def _():
        a = acc_ref[...]
        if epilogue == "relu2":
            a = jnp.square(jnp.maximum(a, 0.0))
        elif epilogue == "residual":
            a = a + r_ref[...]
        o_ref[...] = a.astype(o_ref.dtype)


def _matmul(x, w, *, epilogue="none", res=None, out_dtype=F32, name="matmul"):
    m, kdim = x.shape
    n = w.shape[1]
    tm = _pick_tile(m, 1024, 8)
    tn = _pick_tile(n, 2048, LANE)
    tk = _pick_tile(kdim, 1024, LANE)
    nk = kdim // tk
    in_specs = [pl.BlockSpec((tm, tk), lambda i, j, k: (i, k)), pl.BlockSpec((tk, tn), lambda i, j, k: (k, j))]
    args = [x, w]
    if epilogue == "residual":
        in_specs.append(pl.BlockSpec((tm, tn), lambda i, j, k: (i, j)))
        args.append(res)
    return pl.pallas_call(
        functools.partial(_mm_body, nk=nk, epilogue=epilogue),
        grid=(m // tm, n // tn, nk),
        in_specs=in_specs,
        out_specs=pl.BlockSpec((tm, tn), lambda i, j, k: (i, j)),
        out_shape=jax.ShapeDtypeStruct((m, n), out_dtype),
        scratch_shapes=[pltpu.VMEM((tm, tn), F32)],
        compiler_params=pltpu.CompilerParams(
            dimension_semantics=("parallel", "parallel", "arbitrary"), vmem_limit_bytes=VMEM_LIMIT),
        name=name,
    )(*args)


def _merge_body(o_ref, w_ref, g_ref, out_ref, acc_ref):
    b = pl.program_id(2)

    @pl.when(b == 0)
    def _():
        acc_ref[...] = jnp.zeros_like(acc_ref)

    acc_ref[...] += jax.nn.sigmoid(g_ref[...]) * jnp.dot(o_ref[0], w_ref[0], preferred_element_type=F32)

    @pl.when(b == N_BRANCH - 1)
    def _():
        out_ref[...] = acc_ref[...].astype(out_ref.dtype)


def _merge_branches(o_stack, w_br, h, lay):
    _, m, mix = o_stack.shape
    d = lay.d_model
    tm = _pick_tile(m, 1024, 8)
    tn = _pick_tile(math.gcd(d, lay.off[22]), 1024, LANE)
    g0, gstep = lay.off[22] // tn, d // tn
    return pl.pallas_call(
        _merge_body,
        grid=(m // tm, d // tn, N_BRANCH),
        in_specs=[pl.BlockSpec((1, tm, mix), lambda i, j, b: (b, i, 0)),
                  pl.BlockSpec((1, mix, tn), lambda i, j, b: (b, 0, j)),
                  pl.BlockSpec((tm, tn), lambda i, j, b: (i, g0 + b * gstep + j))],
        out_specs=pl.BlockSpec((tm, tn), lambda i, j, b: (i, j)),
        out_shape=jax.ShapeDtypeStruct((m, d), BF16),
        scratch_shapes=[pltpu.VMEM((tm, tn), F32)],
        compiler_params=pltpu.CompilerParams(
            dimension_semantics=("parallel", "parallel", "arbitrary"), vmem_limit_bytes=VMEM_LIMIT),
        name="merge_branches",
    )(o_stack, w_br, h)


def _l2norm(x):
    return x * lax.rsqrt(jnp.sum(x * x, axis=-1, keepdims=True) + EPS)


def _rms(x, g):
    y = x * lax.rsqrt(jnp.mean(x * x, axis=-1, keepdims=True) + EPS)
    return y * g


def _chunk_len(t, c):
    return c if t % c == 0 else t


def _rope(x, pos):
    half = x.shape[-1] // 2
    freq = jnp.exp(-math.log(ROPE_BASE) * jnp.arange(half, dtype=F32) / half)
    ang = pos.astype(F32)[:, None] * freq
    cos, sin = jnp.cos(ang)[None, :, None, :], jnp.sin(ang)[None, :, None, :]
    x1, x2 = x[..., :half], x[..., half:]
    return jnp.concatenate([x1 * cos - x2 * sin, x1 * sin + x2 * cos], axis=-1)


def _gated_delta_rule(q, k, v, g, beta, s0):
    b, t, h, _ = q.shape
    dv = v.shape[-1]
    c = _chunk_len(t, GDN_CHUNK)
    n = t // c
    ch = lambda a: jnp.swapaxes(a.reshape(b, n, c, h, *a.shape[3:]), 2, 3)
    q, k, v, g, beta = ch(q), ch(k), ch(v), ch(g), ch(beta)
    gc = jnp.cumsum(g, axis=-1)
    incl = jnp.tril(jnp.ones((c, c), bool))
    strict = jnp.tril(jnp.ones((c, c), F32), -1)
    decay = jnp.exp(jnp.where(incl, gc[..., :, None] - gc[..., None, :], -jnp.inf))
    kb = k * beta[..., None]
    a_mat = jnp.einsum('bnhid,bnhjd->bnhij', kb, k) * decay * strict
    eye = jnp.eye(c, dtype=F32)
    t_mat = lax.linalg.triangular_solve(a_mat + eye, jnp.broadcast_to(eye, a_mat.shape), left_side=True, lower=True)
    u = jnp.einsum('bnhij,bnhjd->bnhid', t_mat, v * beta[..., None])
    w = jnp.einsum('bnhij,bnhjd->bnhid', t_mat, kb * jnp.exp(gc)[..., None])
    qk = jnp.einsum('bnhid,bnhjd->bnhij', q, k) * decay
    qg = q * jnp.exp(gc)[..., None]
    kd = k * jnp.exp(gc[..., -1:] - gc)[..., None]
    glast = jnp.exp(gc[..., -1])

    def step(s, xs):
        u_i, w_i, qk_i, qg_i, kd_i, gl_i = xs
        v_new = u_i - jnp.einsum('bhcd,bhde->bhce', w_i, s)
        o = jnp.einsum('bhcd,bhde->bhce', qg_i, s) + jnp.einsum('bhij,bhje->bhie', qk_i, v_new)
        s = s * gl_i[..., None, None] + jnp.einsum('bhcd,bhce->bhde', kd_i, v_new)
        return s, o

    xs = tuple(jnp.moveaxis(a, 1, 0) for a in (u, w, qk, qg, kd, glast))
    s, o = lax.scan(step, s0, xs)
    return o.transpose(1, 0, 3, 2, 4).reshape(b, t, h, dv), s


def _gdn(qkv, z, b_raw, a_raw, conv_buf, s0, conv_w, a_log, dt_bias, norm_g, nh):
    b, t, _ = qkv.shape
    xc = jnp.concatenate([conv_buf, qkv], axis=1)
    y = jax.nn.silu(sum(xc[:, j:j + t] * conv_w[j] for j in range(GDN_CONV)))
    q, k, v = [a.reshape(b, t, nh, HEAD_DIM) for a in jnp.split(y, 3, axis=-1)]
    q = _l2norm(q) * HEAD_DIM ** -0.5
    k = _l2norm(k)
    beta = jax.nn.sigmoid(b_raw)
    g = -jnp.exp(a_log) * jax.nn.softplus(a_raw + dt_bias)
    o, s = _gated_delta_rule(q, k, v, g, beta, s0)
    o = _rms(o, norm_g) * jax.nn.silu(z.reshape(b, t, nh, HEAD_DIM))
    return o.reshape(b, t, nh * HEAD_DIM), xc[:, t:], s


def _retention(q, k, v, gate, pos, s0, gn_g, nh):
    b, t, _ = q.shape
    h, d = nh, HEAD_DIM
    q = _rope(q.reshape(b, t, h, d), pos)
    k = _rope(k.reshape(b, t, h, d), pos) * d ** -0.5
    v = v.reshape(b, t, h, d)
    lg = jnp.log1p(-jnp.exp2(-5.0 - jnp.arange(h, dtype=F32)))
    c = _chunk_len(t, RET_CHUNK)
    n = t // c
    qc, kc, vc = (a.reshape(b, n, c, h, d) for a in (q, k, v))
    i = jnp.arange(c, dtype=F32)
    diff = i[:, None] - i[None, :]
    dmat = jnp.where(diff >= 0, jnp.exp(jnp.maximum(diff, 0.0)[None] * lg[:, None, None]), 0.0)
    o_in = jnp.einsum('bnhij,bnjhe->bnihe', jnp.einsum('bnihd,bnjhd->bnhij', qc, kc) * dmat, vc)
    q_dec = qc * jnp.exp((i + 1.0)[:, None] * lg)[:, :, None]
    k_dec = kc * jnp.exp((c - 1.0 - i)[:, None] * lg)[:, :, None]
    kv = jnp.einsum('bnjhd,bnjhe->bnhde', k_dec, vc)
    chunk_decay = jnp.exp(c * lg)[:, None, None]

    def step(s, xs):
        qd_i, kv_i = xs
        o = jnp.einsum('bihd,bhde->bihe', qd_i, s)
        return s * chunk_decay + kv_i, o

    s, o_x = lax.scan(step, s0, (jnp.moveaxis(q_dec, 1, 0), jnp.moveaxis(kv, 1, 0)))
    o = (o_in + jnp.moveaxis(o_x, 0, 1)).reshape(b, t, h, d)
    oc = o - jnp.mean(o, axis=-1, keepdims=True)
    o = oc * lax.rsqrt(jnp.mean(oc * oc, axis=-1, keepdims=True) + EPS)
    return jax.nn.silu(gate) * (o.reshape(b, t, h * d) * gn_g), s


INT_MIN = -2 ** 31
PAD_SCORE = -3e38
_NT = (((1,), (1,)), ((), ()))


def _col_block(off, width):
    assert off % width == 0, (off, width)
    return off // width


def _sortable(x):
    b = pltpu.bitcast(x, jnp.int32)
    return b ^ ((b >> 31) & 0x7FFFFFFF)


def _iota(shape, dim):
    return lax.broadcasted_iota(jnp.int32, shape, dim)


def _kth_largest(key_ref, nc, k):
    def count_ge(t):
        acc = jnp.where(key_ref[0] >= t, 1.0, 0.0)
        for c in range(1, nc):
            acc = acc + jnp.where(key_ref[c] >= t, 1.0, 0.0)
        return jnp.sum(acc, axis=-1, keepdims=True)

    tq = key_ref.shape[1]
    zero = jnp.zeros((tq, 1), jnp.int32)
    base = jnp.where(count_ge(zero) >= k, zero, jnp.full((tq, 1), INT_MIN, jnp.int32))

    def body(i, base):
        cand = base | (jnp.int32(1) << (30 - i))
        return jnp.where(count_ge(cand) >= k, cand, base)

    return lax.fori_loop(0, 31, body, base)


def _topk_blocks(key_ref, nc, k):
    _, tq, lc = key_ref.shape
    thr = _kth_largest(key_ref, nc, k)
    n_gt = jnp.where(key_ref[0] > thr, 1.0, 0.0)
    for c in range(1, nc):
        n_gt = n_gt + jnp.where(key_ref[c] > thr, 1.0, 0.0)
    need = k - jnp.sum(n_gt, axis=-1, keepdims=True)
    tri = jnp.where(_iota((LANE, LANE), 0) <= _iota((LANE, LANE), 1), 1.0, 0.0).astype(BF16)
    carry = jnp.zeros((tq, 1), F32)
    for c in range(nc):
        for j in range(lc // LANE):
            kk = key_ref[c, :, j * LANE:(j + 1) * LANE]
            eq = kk == thr
            eqf = jnp.where(eq, 1.0, 0.0)
            prefix = jnp.dot(eqf.astype(BF16), tri, preferred_element_type=F32) + carry
            carry = carry + jnp.sum(eqf, axis=-1, keepdims=True)
            yield c, j, (kk > thr) | (eq & (prefix <= need))


def _dsa_select_body(qi_ref, sm_ref, ki_ref, o_ref, qh32_ref, qh_ref, d_ref, key_ref, *, tq, lc, nc, pos0, topk, wi_off):
    qpos = pos0 + pl.program_id(1) * tq + _iota((tq, 1), 0)
    qi = qi_ref[0]
    for h in range(IDX_HEADS):
        qh32_ref[h * tq:(h + 1) * tq, :] = qi[:, h * IDX_DIM:(h + 1) * IDX_DIM]
    qh_ref[...] = qh32_ref[...].astype(BF16)
    sm = sm_ref[0]
    for c in range(nc):
        kic = ki_ref[0, c * lc:(c + 1) * lc, :IDX_DIM].astype(BF16)
        d_ref[...] = lax.dot_general(qh_ref[...], kic, _NT, preferred_element_type=F32)
        score = jnp.zeros((tq, lc), F32)
        for h in range(IDX_HEADS):
            score = score + sm[:, wi_off + h:wi_off + h + 1] * jnp.maximum(d_ref[h * tq:(h + 1) * tq, :], 0.0)
        kpos = c * lc + _iota((tq, lc), 1)
        key_ref[c] = _sortable(jnp.where(kpos <= qpos, score, NEG))
    pieces = []
    for c, j, sel in _topk_blocks(key_ref, nc, topk):
        kpos = c * lc + j * LANE + _iota((tq, LANE), 1)
        pieces.append(jnp.where(sel & (kpos <= qpos), 1.0, 0.0))
        if len(pieces) == lc // LANE:
            o_ref[0, c] = jnp.concatenate(pieces, axis=1).astype(BF16)
            pieces = []


def _dsa_select(hq, qi_off, sm_off, rows, ki_off, *, tq, lc, pos0, l_real, wi_off):
    b, t, _ = hq.shape
    l = rows.shape[1]
    nc = l // lc
    topk = min(DSA_TOPK_MAX, l_real // 4)
    qw = IDX_HEADS * IDX_DIM
    return pl.pallas_call(
        functools.partial(_dsa_select_body, tq=tq, lc=lc, nc=nc, pos0=pos0, topk=topk, wi_off=wi_off),
        grid=(b, t // tq),
        in_specs=[pl.BlockSpec((1, tq, qw), lambda i, q: (i, q, _col_block(qi_off, qw))),
                  pl.BlockSpec((1, tq, LANE), lambda i, q: (i, q, _col_block(sm_off, LANE))),
                  pl.BlockSpec((1, l, LANE), lambda i, q: (i, 0, _col_block(ki_off, LANE)))],
        out_specs=pl.BlockSpec((1, nc, tq, lc), lambda i, q: (i, 0, q, 0)),
        out_shape=jax.ShapeDtypeStruct((b, nc, t, lc), BF16),
        scratch_shapes=[pltpu.VMEM((IDX_HEADS * tq, IDX_DIM), F32), pltpu.VMEM((IDX_HEADS * tq, IDX_DIM), BF16),
                        pltpu.VMEM((IDX_HEADS * tq, lc), F32), pltpu.VMEM((nc, tq, lc), jnp.int32)],
        compiler_params=pltpu.CompilerParams(dimension_semantics=("parallel", "arbitrary"),
                                             vmem_limit_bytes=VMEM_LIMIT),
        name="dsa_select",
    )(hq, hq, rows)


def _nsa_select_body(q_ref, kc0_ref, kc1_ref, vc0_ref, vc1_ref, wk_ref, wv_ref, bk_ref, bv_ref, ov_ref, e_ref,
                     ocmp_ref, mask_ref, ck_ref, cv_ref, cacc_ref, key_ref, *, tq, lc, nc, pos0, l16, ns, n_sel, grp):
    qb = pl.program_id(1)

    @pl.when(qb == 0)
    def _():
        for srcs, w_ref, b_ref, dst in (((kc0_ref, kc1_ref), wk_ref, bk_ref, ck_ref),
                                        ((vc0_ref, vc1_ref), wv_ref, bv_ref, cv_ref)):
            for g in range(KV_HEADS):
                for c in range(CMP_STRIDE):
                    x = srcs[g][0, pl.ds(c, l16, stride=CMP_STRIDE), :].astype(BF16)
                    part = jnp.dot(x, w_ref[c], preferred_element_type=F32)
                    if c == 0:
                        cacc_ref[...] = part
                    else:
                        cacc_ref[...] += part
                acc = cacc_ref[...]
                summ = acc[:, :HEAD_DIM] + pltpu.roll(acc[:, HEAD_DIM:], l16 - 1, 0) + b_ref[...]
                dst[g] = summ.astype(BF16)

    qpos = pos0 + qb * tq + _iota((tq, 1), 0)
    vis = (_iota((tq, l16), 1) * CMP_STRIDE + (CMP_BLOCK - 1)) <= qpos
    jcol = _iota((tq, ns), 1)
    cur = qpos >> int(math.log2(SEL_BLOCK))
    forced = (jcol == 0) | (jcol == cur) | (jcol == cur - 1)
    admissible = jcol * SEL_BLOCK <= qpos
    scale = HEAD_DIM ** -0.5
    for g in range(KV_HEADS):
        imp_c = jnp.zeros((tq, l16), F32)
        for r in range(grp):
            hd = g * grp + r
            q = q_ref[0][:, hd * HEAD_DIM:(hd + 1) * HEAD_DIM].astype(BF16)
            s = lax.dot_general(q, ck_ref[g], _NT, preferred_element_type=F32) * scale
            s = jnp.where(vis, s, NEG)
            e = jnp.exp(s - jnp.max(s, axis=-1, keepdims=True))
            p = jnp.where(vis, e / jnp.sum(e, axis=-1, keepdims=True), 0.0)
            imp_c = imp_c + p
            ocmp_ref[0, :, hd * HEAD_DIM:(hd + 1) * HEAD_DIM] = jnp.dot(
                p.astype(BF16), cv_ref[g], preferred_element_type=F32)
        hi = imp_c.astype(BF16)
        rem = imp_c - hi.astype(F32)
        mid = rem.astype(BF16)
        lo = (rem - mid.astype(F32)).astype(BF16)
        ov = ov_ref[...]
        imp = (jnp.dot(hi, ov, preferred_element_type=F32) + jnp.dot(mid, ov, preferred_element_type=F32)
               + jnp.dot(lo, ov, preferred_element_type=F32))
        imp = jnp.where(admissible, jnp.where(forced, imp + FORCE_BONUS, imp), NEG)
        imp = jnp.where(jcol < n_sel, imp, PAD_SCORE)
        key_ref[0] = _sortable(imp)
        sel = jnp.concatenate([jnp.where(s_, 1.0, 0.0) for _, _, s_ in _topk_blocks(key_ref, 1, min(SEL_TOPN, n_sel))],
                              axis=1).astype(BF16)
        for c in range(nc):
            tok = jnp.dot(sel, e_ref[:, c * lc:(c + 1) * lc], preferred_element_type=F32)
            kpos = c * lc + _iota((tq, lc), 1)
            mask_ref[g, c] = jnp.where((tok > 0.5) & (kpos <= qpos), 1.0, 0.0).astype(BF16)


def _nsa_select(hq, q_off, rows, kc_off, vc_off, cwk, cwv, pek, pev, *, tq, lc, pos0, l_real, nh):
    b, t, _ = hq.shape
    l = rows.shape[1]
    nc = l // lc
    grp = nh // KV_HEADS
    n_cmp = (l_real - CMP_BLOCK) // CMP_STRIDE + 1
    l16 = n_cmp + 1
    n_sel = -(-l_real // SEL_BLOCK)
    ns = _round_up(n_sel, LANE)
    pack = lambda w: jnp.concatenate([w[:CMP_STRIDE], w[CMP_STRIDE:]], axis=-1).astype(BF16)
    bias = lambda pe, w: jnp.einsum('cd,cde->e', pe, w, precision=lax.Precision.HIGHEST).reshape(1, HEAD_DIM)
    c0 = np.arange(l16)[:, None] * CMP_STRIDE
    s0 = np.arange(ns)[None, :] * SEL_BLOCK
    overlap = jnp.asarray((c0 < s0 + SEL_BLOCK) & (c0 + CMP_BLOCK > s0), BF16)
    expand = jnp.asarray(np.arange(l)[None, :] // SEL_BLOCK == np.arange(ns)[:, None], BF16)
    qw = nh * HEAD_DIM
    lrows = CMP_STRIDE * l16
    const = lambda shape: pl.BlockSpec(shape, lambda i, q: (0,) * len(shape))
    head_rows = lambda off, g: pl.BlockSpec((1, lrows, HEAD_DIM), lambda i, q: (i, 0, _col_block(off, HEAD_DIM) + g))
    assert KV_HEADS == 2
    return pl.pallas_call(
        functools.partial(_nsa_select_body, tq=tq, lc=lc, nc=nc, pos0=pos0, l16=l16, ns=ns, n_sel=n_sel, grp=grp),
        grid=(b, t // tq),
        in_specs=[pl.BlockSpec((1, tq, qw), lambda i, q: (i, q, _col_block(q_off, qw))),
                  head_rows(kc_off, 0), head_rows(kc_off, 1), head_rows(vc_off, 0), head_rows(vc_off, 1),
                  const((CMP_STRIDE, HEAD_DIM, 2 * HEAD_DIM)), const((CMP_STRIDE, HEAD_DIM, 2 * HEAD_DIM)),
                  const((1, HEAD_DIM)), const((1, HEAD_DIM)), const((l16, ns)), const((ns, l))],
        out_specs=[pl.BlockSpec((1, tq, qw), lambda i, q: (i, q, 0)),
                   pl.BlockSpec((KV_HEADS, nc, tq, lc), lambda i, q: (i, 0, q, 0))],
        out_shape=[jax.ShapeDtypeStruct((b, t, qw), F32),
                   jax.ShapeDtypeStruct((b * KV_HEADS, nc, t, lc), BF16)],
        scratch_shapes=[pltpu.VMEM((KV_HEADS, l16, HEAD_DIM), BF16), pltpu.VMEM((KV_HEADS, l16, HEAD_DIM), BF16),
                        pltpu.VMEM((l16, 2 * HEAD_DIM), F32), pltpu.VMEM((1, tq, ns), jnp.int32)],
        compiler_params=pltpu.CompilerParams(dimension_semantics=("parallel", "arbitrary"),
                                             vmem_limit_bytes=VMEM_LIMIT),
        name="nsa_select",
    )(hq, rows, rows, rows, rows, pack(cwk), pack(cwv), bias(pek, cwk), bias(pev, cwv), overlap, expand)


def _attend_body(*refs, banded, tq, lc, nc, pos0, kpos0, grp):
    if banded:
        q_ref, k_ref, v_ref, o_ref, s_ref, acc_ref = refs
    else:
        q_ref, k_ref, v_ref, m_ref, o_ref, s_ref, acc_ref = refs
    q_first = pos0 + pl.program_id(2) * tq
    qpos = q_first + _iota((tq, 1), 0)
    c_hi = jnp.minimum(nc, (q_first + tq - 1 - kpos0) // lc + 1)
    c_lo = jnp.maximum(q_first - WINDOW - kpos0, 0) // lc if banded else 0
    scale = HEAD_DIM ** -0.5
    qs = [q_ref[0][:, r * HEAD_DIM:(r + 1) * HEAD_DIM].astype(BF16) for r in range(grp)]

    def scores(c, ms):
        kc = k_ref[0, pl.ds(pl.multiple_of(c * lc, lc), lc), :].astype(BF16)
        if banded:
            kpos = kpos0 + c * lc + _iota((tq, lc), 1)
            valid = (kpos <= qpos) & (qpos - kpos <= WINDOW)
        else:
            valid = m_ref[0, c].astype(F32) > 0.5
        out = []
        for r in range(grp):
            s = jnp.where(valid, lax.dot_general(qs[r], kc, _NT, preferred_element_type=F32) * scale, NEG)
            s_ref[r, c] = s
            out.append(jnp.maximum(ms[r], jnp.max(s, axis=-1, keepdims=True)))
        return tuple(out)

    ms = lax.fori_loop(c_lo, c_hi, scores, tuple(jnp.full((tq, 1), NEG, F32) for _ in range(grp)))
    acc_ref[...] = jnp.zeros_like(acc_ref)

    def values(c, ls):
        vc = v_ref[0, pl.ds(pl.multiple_of(c * lc, lc), lc), :].astype(BF16)
        out = []
        for r in range(grp):
            p = jnp.exp(s_ref[r, c] - ms[r])
            acc_ref[r] += jnp.dot(p.astype(BF16), vc, preferred_element_type=F32)
            out.append(ls[r] + jnp.sum(p, axis=-1, keepdims=True))
        return tuple(out)

    ls = lax.fori_loop(c_lo, c_hi, values, tuple(jnp.zeros((tq, 1), F32) for _ in range(grp)))
    for r in range(grp):
        o_ref[0, :, r * HEAD_DIM:(r + 1) * HEAD_DIM] = acc_ref[r] / ls[r]


def _attend(hq, q_off, kv, k_off, v_off, mask, *, tq, lc, pos0, kpos0, nh, mask_per_group=False, name="attend"):
    b, t, _ = hq.shape
    l = kv.shape[1]
    nc = l // lc
    grp = nh // KV_HEADS
    qw = grp * HEAD_DIM
    banded = mask is None
    in_specs = [pl.BlockSpec((1, tq, qw), lambda i, g, q: (i, q, _col_block(q_off, qw) + g)),
                pl.BlockSpec((1, l, HEAD_DIM), lambda i, g, q: (i, 0, _col_block(k_off, HEAD_DIM) + g)),
                pl.BlockSpec((1, l, HEAD_DIM), lambda i, g, q: (i, 0, _col_block(v_off, HEAD_DIM) + g))]
    args = [hq, kv, kv]
    if not banded:
        if mask_per_group:
            in_specs.append(pl.BlockSpec((1, nc, tq, lc), lambda i, g, q: (i * KV_HEADS + g, 0, q, 0)))
        else:
            in_specs.append(pl.BlockSpec((1, nc, tq, lc), lambda i, g, q: (i, 0, q, 0)))
        args.append(mask)
    return pl.pallas_call(
        functools.partial(_attend_body, banded=banded, tq=tq, lc=lc, nc=nc, pos0=pos0, kpos0=kpos0, grp=grp),
        grid=(b, KV_HEADS, t // tq),
        in_specs=in_specs,
        out_specs=pl.BlockSpec((1, tq, qw), lambda i, g, q: (i, q, g)),
        out_shape=jax.ShapeDtypeStruct((b, t, nh * HEAD_DIM), F32),
        scratch_shapes=[pltpu.VMEM((grp, nc, tq, lc), F32), pltpu.VMEM((grp, tq, HEAD_DIM), F32)],
        compiler_params=pltpu.CompilerParams(dimension_semantics=("parallel", "parallel", "arbitrary"),
                                             vmem_limit_bytes=VMEM_LIMIT),
        name=name,
    )(*args)


def _nsa_combine_body(a_ref, b_ref, c_ref, sm_ref, o_ref, *, gate_off, nh):
    gate = jax.nn.sigmoid(sm_ref[...])
    for hd in range(nh):
        sl = slice(hd * HEAD_DIM, (hd + 1) * HEAD_DIM)
        col = lambda br: gate[:, gate_off + br * nh + hd:gate_off + br * nh + hd + 1]
        o_ref[:, sl] = col(0) * a_ref[:, sl] + col(1) * b_ref[:, sl] + col(2) * c_ref[:, sl]


def _nsa_combine(o_cmp, o_sel, o_win, h, sm_off, gate_off, nh):
    m, w = o_cmp.shape
    tm = _pick_tile(m, 512, 8)
    blk = pl.BlockSpec((tm, w), lambda i: (i, 0))
    return pl.pallas_call(
        functools.partial(_nsa_combine_body, gate_off=gate_off, nh=nh),
        grid=(m // tm,),
        in_specs=[blk, blk, blk, pl.BlockSpec((tm, LANE), lambda i: (i, _col_block(sm_off, LANE)))],
        out_specs=blk,
        out_shape=jax.ShapeDtypeStruct((m, w), F32),
        compiler_params=pltpu.CompilerParams(dimension_semantics=("parallel",), vmem_limit_bytes=VMEM_LIMIT),
        name="nsa_combine",
    )(o_cmp, o_sel, o_win, h)


def _sparse_mixers(h, hq, kv, kv_off, win, win_off, win_pos0, lay, cw, *, tq, pos0, l_real, t_real):
    cwk, cwv, pek, pev = cw
    nh = lay.nh
    b = hq.shape[0]
    lc = _pick_tile(kv.shape[1], 32 * 1024 // tq, LANE)
    lcw = _pick_tile(win.shape[1], 16 * 1024 // tq, LANE)
    sm_off = lay.off[2]
    offs = np.concatenate([[0], np.cumsum(PAGED_SIZES)]).tolist()
    common = dict(tq=tq, pos0=pos0, nh=nh)
    o_cmp, sel_mask = _nsa_select(hq, lay.off[8], kv, kv_off + offs[0], kv_off + offs[1], cwk, cwv, pek, pev,
                                  lc=lc, l_real=l_real, **common)
    o_sel = _attend(hq, lay.off[8], kv, kv_off + offs[2], kv_off + offs[3], sel_mask, lc=lc, kpos0=0,
                    mask_per_group=True, name="nsa_selected", **common)
    o_win = _attend(hq, lay.off[8], win, win_off, win_off + KVW, None, lc=lcw, kpos0=win_pos0,
                    name="nsa_window", **common)
    dsa_mask = _dsa_select(hq, lay.off[19], sm_off, kv, kv_off + offs[6], tq=tq, lc=lc, pos0=pos0, l_real=l_real,
                           wi_off=lay.off[21] - sm_off)
    o_d = _attend(hq, lay.off[16], kv, kv_off + offs[4], kv_off + offs[5], dsa_mask, lc=lc, kpos0=0,
                  name="dsa_attend", **common)
    unpad = lambda o: o[:, :t_real].reshape(b * t_real, nh * HEAD_DIM)
    o_c = _nsa_combine(unpad(o_cmp), unpad(o_sel), unpad(o_win), h, sm_off, lay.off[15] - sm_off, nh)
    return o_c, unpad(o_d)


def _layer(x, q_pos, lw, past, lay):
    (g_mix, g_mlp, w_in_p, conv_w, a_log, dt_bias, gdn_g, ret_g, cwk, cwv, pek, pev, w_br, w_o, w_up_l, w_down_l) = lw
    conv_buf, s_gdn, s_ret, win_buf, past_rows = past
    b, t, d = x.shape
    m = b * t
    mix, nh = lay.mix, lay.nh
    x2d = x.reshape(m, d)
    h = _matmul(_rmsnorm(x2d, g_mix, BF16), w_in_p, name="in_proj")
    h3 = h.reshape(b, t, lay.width)
    seg = lambda idx: h3[:, :, lay.off[idx]:lay.off[idx] + lay.in_sizes[idx]]

    o_a, conv_new, s_gdn_new = _gdn(seg(0), seg(1), seg(2), seg(3), conv_buf, s_gdn, conv_w, a_log, dt_bias, gdn_g, nh)
    o_b, s_ret_new = _retention(seg(4), seg(5), seg(6), seg(7), q_pos, s_ret, ret_g, nh)

    new_rows = h3[:, :, lay.off[9]:lay.off[9] + PAGED_W]
    win_rows = h3[:, :, lay.off[13]:lay.off[13] + 2 * KVW]
    cw = (cwk, cwv, pek, pev)
    if past_rows is None:
        win_new = win_rows[:, t - min(WINDOW, t):]
        o_c, o_d = _sparse_mixers(h, h3, h3, lay.off[9], h3, lay.off[13], 0, lay, cw,
                                  tq=Q_BLOCK, pos0=0, l_real=t, t_real=t)
    else:
        past_len, wb = past_rows.shape[1], win_buf.shape[1]
        tq = _round_up(t, 8)
        pad_rows = lambda a, n: jnp.pad(a, ((0, 0), (0, n - a.shape[1]), (0, 0)))
        l_real = past_len + t
        kv = jnp.concatenate([past_rows, new_rows], axis=1)
        kv = jnp.pad(kv, ((0, 0), (0, _round_up(l_real, 2 * LANE) - l_real), (0, _round_up(PAGED_W, LANE) - PAGED_W)))
        win_all = jnp.concatenate([win_buf, win_rows], axis=1)
        win_new = win_all[:, t:]
        o_c, o_d = _sparse_mixers(h, pad_rows(h3, tq), kv, 0, pad_rows(win_all, _round_up(wb + t, LANE)), 0,
                                  past_len - wb, lay, cw, tq=tq, pos0=past_len, l_real=l_real, t_real=t)

    o_stack = jnp.stack([o.reshape(m, mix) for o in (o_a, o_b, o_c, o_d)]).astype(BF16)
    mixed = _merge_branches(o_stack, w_br, h, lay)
    x2 = _matmul(mixed, w_o, epilogue="residual", res=x2d, name="out_proj")
    up = _matmul(_rmsnorm(x2, g_mlp, BF16), w_up_l, epilogue="relu2", out_dtype=BF16, name="ffn_up")
    x3 = _matmul(up, w_down_l, epilogue="residual", res=x2, name="ffn_down")
    return x3.reshape(b, t, d), (new_rows, conv_new, s_gdn_new, s_ret_new, win_new)


def kernel(x_prompt, x_sample, cache_kv, cache_nsa_window, state_gdn, state_gdn_conv, state_retention, page_table, norm_mix, norm_mlp, norm_final, w_in, gdn_conv_w, gdn_a_log, gdn_dt_bias, gdn_norm, ret_norm, nsa_cmp_wk, nsa_cmp_wv, nsa_cmp_pe_k, nsa_cmp_pe_v, w_branch, w_out, w_up, w_down):
    bp, tp, d = x_prompt.shape
    bs, ts, _ = x_sample.shape
    depth = w_in.shape[0]
    lay = _Layout(d)
    past_len = page_table.shape[1] * cache_kv.shape[1]
    pos_p = jnp.arange(tp, dtype=jnp.int32)
    pos_s = past_len + jnp.arange(ts, dtype=jnp.int32)
    zero_conv = jnp.zeros((bp, GDN_CONV - 1, 3 * lay.mix), F32)
    zero_state = jnp.zeros((bp, lay.nh, HEAD_DIM, HEAD_DIM), F32)
    w_in_p = _pack_w_in(w_in, lay)
    w_br_b, w_o_b, w_up_b, w_down_b = (a.astype(BF16) for a in (w_branch, w_out, w_up, w_down))
    hp, hs = x_prompt, x_sample
    out_p, out_s = [], []
    for l in range(depth):
        lw = (norm_mix[l], norm_mlp[l], w_in_p[l], gdn_conv_w[l], gdn_a_log[l], gdn_dt_bias[l], gdn_norm[l],
              ret_norm[l], nsa_cmp_wk[l], nsa_cmp_wv[l], nsa_cmp_pe_k[l], nsa_cmp_pe_v[l], w_br_b[l],
              w_o_b[l], w_up_b[l], w_down_b[l])
        hp, st_p = _layer(hp, pos_p, lw, (zero_conv, zero_state, zero_state, None, None), lay)
        out_p.append(st_p)
        past_rows = cache_kv[page_table, :, l].reshape(bs, past_len, PAGED_W)
        past_s = (state_gdn_conv[:, l], state_gdn[:, l], state_retention[:, l], cache_nsa_window[:, l], past_rows)
        hs, st_s = _layer(hs, pos_s, lw, past_s, lay)
        out_s.append(st_s)
    y_prompt = _rmsnorm(hp.reshape(bp * tp, d), norm_final, F32).reshape(bp, tp, d)
    y_sample = _rmsnorm(hs.reshape(bs * ts, d), norm_final, F32).reshape(bs, ts, d)
    stack = lambda outs, j, axis: jnp.stack([o[j] for o in outs], axis=axis)
    return (y_prompt, y_sample,
            stack(out_p, 0, 2), stack(out_s, 0, 2),
            stack(out_p, 2, 1), stack(out_s, 2, 1),
            stack(out_p, 1, 1), stack(out_s, 1, 1),
            stack(out_p, 3, 1), stack(out_s, 3, 1),
            stack(out_p, 4, 1), stack(out_s, 4, 1))
```

```python
import functools
import math

import jax
import jax.numpy as jnp
import numpy as np
from jax import lax
from jax.experimental import pallas as pl
from jax.experimental.pallas import tpu as pltpu

HEAD_DIM = 128
N_BRANCH = 4
EPS = 1e-6
NEG = -1e30
GDN_CONV = 4
GDN_CHUNK = 64
RET_CHUNK = 64
ROPE_BASE = 10000.0
KV_HEADS = 2
CMP_BLOCK = 32
CMP_STRIDE = 16
SEL_BLOCK = 64
SEL_TOPN = 16
WINDOW = 512
FORCE_BONUS = 1e4
IDX_HEADS = 16
IDX_DIM = 64
DSA_TOPK_MAX = 256
Q_BLOCK = 128
KVW = KV_HEADS * HEAD_DIM
PAGED_SIZES = (KVW, KVW, KVW, KVW, KVW, KVW, IDX_DIM)
PAGED_W = sum(PAGED_SIZES)

LANE = 128
MXU_DIM = 256
VMEM_LIMIT = 56 * 1024 * 1024

F32 = jnp.float32
BF16 = jnp.bfloat16


def _round_up(n, m):
    return -(-n // m) * m


def _pick_tile(dim, target, align):
    best = None
    for t in range(align, min(dim, target) + 1, align):
        if dim % t == 0:
            best = t
    return best if best is not None else dim


class _Layout:
    def __init__(self, d_model):
        mix = d_model // N_BRANCH
        nh = mix // HEAD_DIM
        self.d_model, self.mix, self.nh = d_model, mix, nh
        self.in_sizes = (
            3 * mix, mix, nh, nh,
            mix, mix, mix, mix,
            mix, KVW, KVW, KVW, KVW, KVW, KVW, 3 * nh,
            mix, KVW, KVW, IDX_HEADS * IDX_DIM, IDX_DIM, IDX_HEADS,
            N_BRANCH * d_model)
        self.order = [0, 1, 4, 5, 6, 7, 8, 16, 19, 22, 9, 10, 11, 12, 17, 18, 20, None, 13, 14, 2, 3, 15, 21, None]
        self.off = {}
        pos = 0
        self.pads = []
        for idx in self.order:
            if idx is None:
                pad = _round_up(pos, LANE) - pos
                self.pads.append(pad)
                pos += pad
            else:
                self.off[idx] = pos
                pos += self.in_sizes[idx]
        self.width = pos


def _pack_w_in(w_in, lay):
    starts = np.concatenate([[0], np.cumsum(lay.in_sizes)]).tolist()
    parts, pads = [], iter(lay.pads)
    for idx in lay.order:
        if idx is None:
            parts.append(jnp.zeros(w_in.shape[:2] + (next(pads),), BF16))
        else:
            parts.append(w_in[:, :, starts[idx]:starts[idx + 1]].astype(BF16))
    return jnp.concatenate(parts, axis=-1)


def _rmsnorm_body(x_ref, g_ref, o_ref):
    x = x_ref[...]
    y = x * lax.rsqrt(jnp.mean(x * x, axis=-1, keepdims=True) + EPS)
    o_ref[...] = (y * g_ref[...]).astype(o_ref.dtype)


def _rmsnorm(x, g, out_dtype):
    m, d = x.shape
    tm = _pick_tile(m, 256, 8)
    return pl.pallas_call(
        _rmsnorm_body,
        grid=(m // tm,),
        in_specs=[pl.BlockSpec((tm, d), lambda i: (i, 0)), pl.BlockSpec((1, d), lambda i: (0, 0))],
        out_specs=pl.BlockSpec((tm, d), lambda i: (i, 0)),
        out_shape=jax.ShapeDtypeStruct((m, d), out_dtype),
        compiler_params=pltpu.CompilerParams(dimension_semantics=("parallel",), vmem_limit_bytes=VMEM_LIMIT),
        name="rmsnorm",
    )(x, g.reshape(1, d).astype(F32))


def _mm_body(*refs, nk, epilogue):
    if epilogue == "residual":
        x_ref, w_ref, r_ref, o_ref, acc_ref = refs
    else:
        x_ref, w_ref, o_ref, acc_ref = refs
    k = pl.program_id(2)

    @pl.when(k == 0)
    def _():
        acc_ref[...] = jnp.zeros_like(acc_ref)

    acc_ref[...] += jnp.dot(x_ref[...], w_ref[...], preferred_element_type=F32)

    @pl.when(k == nk - 1)
    def _():
        a = acc_ref[...]
        if epilogue == "relu2":
            a = jnp.square(jnp.maximum(a, 0.0))
        elif epilogue == "residual":
            a = a + r_ref[...]
        o_ref[...] = a.astype(o_ref.dtype)


def _matmul(x, w, *, epilogue="none", res=None, out_dtype=F32, name="matmul"):
    m, kdim = x.shape
    n = w.shape[1]
    tm = _pick_tile(m, 1024, 8)
    tn = _pick_tile(n, 2304, MXU_DIM) if n % MXU_DIM == 0 else _pick_tile(n, 2304, LANE)
    tk = _pick_tile(kdim, 1024, LANE)
    nk = kdim // tk
    in_specs = [pl.BlockSpec((tm, tk), lambda i, j, k: (i, k)), pl.BlockSpec((tk, tn), lambda i, j, k: (k, j))]
    args = [x, w]
    if epilogue == "residual":
        in_specs.append(pl.BlockSpec((tm, tn), lambda i, j, k: (i, j)))
        args.append(res)
    return pl.pallas_call(
        functools.partial(_mm_body, nk=nk, epilogue=epilogue),
        grid=(m // tm, n // tn, nk),
        in_specs=in_specs,
        out_specs=pl.BlockSpec((tm, tn), lambda i, j, k: (i, j)),
        out_shape=jax.ShapeDtypeStruct((m, n), out_dtype),
        scratch_shapes=[pltpu.VMEM((tm, tn), F32)],
        compiler_params=pltpu.CompilerParams(
            dimension_semantics=("parallel", "parallel", "arbitrary"), vmem_limit_bytes=VMEM_LIMIT),
        name=name,
    )(*args)


def _merge_body(o_ref, w_ref, g_ref, out_ref, acc_ref):
    b = pl.program_id(2)

    @pl.when(b == 0)
    def _():
        acc_ref[...] = jnp.zeros_like(acc_ref)

    acc_ref[...] += jax.nn.sigmoid(g_ref[...]) * jnp.dot(o_ref[0], w_ref[0], preferred_element_type=F32)

    @pl.when(b == N_BRANCH - 1)
    def _():
        out_ref[...] = acc_ref[...].astype(out_ref.dtype)


def _merge_branches(o_stack, w_br, h, lay):
    _, m, mix = o_stack.shape
    d = lay.d_model
    tm = _pick_tile(m, 1024, 8)
    tn = _pick_tile(math.gcd(d, lay.off[22]), 1024, LANE)
    g0, gstep = lay.off[22] // tn, d // tn
    return pl.pallas_call(
        _merge_body,
        grid=(m // tm, d // tn, N_BRANCH),
        in_specs=[pl.BlockSpec((1, tm, mix), lambda i, j, b: (b, i, 0)),
                  pl.BlockSpec((1, mix, tn), lambda i, j, b: (b, 0, j)),
                  pl.BlockSpec((tm, tn), lambda i, j, b: (i, g0 + b * gstep + j))],
        out_specs=pl.BlockSpec((tm, tn), lambda i, j, b: (i, j)),
        out_shape=jax.ShapeDtypeStruct((m, d), BF16),
        scratch_shapes=[pltpu.VMEM((tm, tn), F32)],
        compiler_params=pltpu.CompilerParams(
            dimension_semantics=("parallel", "parallel", "arbitrary"), vmem_limit_bytes=VMEM_LIMIT),
        name="merge_branches",
    )(o_stack, w_br, h)


def _l2norm(x):
    return x * lax.rsqrt(jnp.sum(x * x, axis=-1, keepdims=True) + EPS)


def _rms(x, g):
    y = x * lax.rsqrt(jnp.mean(x * x, axis=-1, keepdims=True) + EPS)
    return y * g


def _chunk_len(t, c):
    return c if t % c == 0 else t


def _rope(x, pos):
    half = x.shape[-1] // 2
    freq = jnp.exp(-math.log(ROPE_BASE) * jnp.arange(half, dtype=F32) / half)
    ang = pos.astype(F32)[:, None] * freq
    cos, sin = jnp.cos(ang)[None, :, None, :], jnp.sin(ang)[None, :, None, :]
    x1, x2 = x[..., :half], x[..., half:]
    return jnp.concatenate([x1 * cos - x2 * sin, x1 * sin + x2 * cos], axis=-1)


def _gated_delta_rule(q, k, v, g, beta, s0):
    b, t, h, _ = q.shape
    dv = v.shape[-1]
    c = _chunk_len(t, GDN_CHUNK)
    n = t // c
    ch = lambda a: jnp.swapaxes(a.reshape(b, n, c, h, *a.shape[3:]), 2, 3)
    q, k, v, g, beta = ch(q), ch(k), ch(v), ch(g), ch(beta)
    gc = jnp.cumsum(g, axis=-1)
    incl = jnp.tril(jnp.ones((c, c), bool))
    strict = jnp.tril(jnp.ones((c, c), F32), -1)
    decay = jnp.exp(jnp.where(incl, gc[..., :, None] - gc[..., None, :], -jnp.inf))
    kb = k * beta[..., None]
    a_mat = jnp.einsum('bnhid,bnhjd->bnhij', kb, k) * decay * strict
    eye = jnp.eye(c, dtype=F32)
    t_mat = lax.linalg.triangular_solve(a_mat + eye, jnp.broadcast_to(eye, a_mat.shape), left_side=True, lower=True)
    u = jnp.einsum('bnhij,bnhjd->bnhid', t_mat, v * beta[..., None])
    w = jnp.einsum('bnhij,bnhjd->bnhid', t_mat, kb * jnp.exp(gc)[..., None])
    qk = jnp.einsum('bnhid,bnhjd->bnhij', q, k) * decay
    qg = q * jnp.exp(gc)[..., None]
    kd = k * jnp.exp(gc[..., -1:] - gc)[..., None]
    glast = jnp.exp(gc[..., -1])

    def step(s, xs):
        u_i, w_i, qk_i, qg_i, kd_i, gl_i = xs
        v_new = u_i - jnp.einsum('bhcd,bhde->bhce', w_i, s)
        o = jnp.einsum('bhcd,bhde->bhce', qg_i, s) + jnp.einsum('bhij,bhje->bhie', qk_i, v_new)
        s = s * gl_i[..., None, None] + jnp.einsum('bhcd,bhce->bhde', kd_i, v_new)
        return s, o

    xs = tuple(jnp.moveaxis(a, 1, 0) for a in (u, w, qk, qg, kd, glast))
    s, o = lax.scan(step, s0, xs)
    return o.transpose(1, 0, 3, 2, 4).reshape(b, t, h, dv), s


def _gdn(qkv, z, b_raw, a_raw, conv_buf, s0, conv_w, a_log, dt_bias, norm_g, nh):
    b, t, _ = qkv.shape
    xc = jnp.concatenate([conv_buf, qkv], axis=1)
    y = jax.nn.silu(sum(xc[:, j:j + t] * conv_w[j] for j in range(GDN_CONV)))
    q, k, v = [a.reshape(b, t, nh, HEAD_DIM) for a in jnp.split(y, 3, axis=-1)]
    q = _l2norm(q) * HEAD_DIM ** -0.5
    k = _l2norm(k)
    beta = jax.nn.sigmoid(b_raw)
    g = -jnp.exp(a_log) * jax.nn.softplus(a_raw + dt_bias)
    o, s = _gated_delta_rule(q, k, v, g, beta, s0)
    o = _rms(o, norm_g) * jax.nn.silu(z.reshape(b, t, nh, HEAD_DIM))
    return o.reshape(b, t, nh * HEAD_DIM), xc[:, t:], s


def _retention(q, k, v, gate, pos, s0, gn_g, nh):
    b, t, _ = q.shape
    h, d = nh, HEAD_DIM
    q = _rope(q.reshape(b, t, h, d), pos)
    k = _rope(k.reshape(b, t, h, d), pos) * d ** -0.5
    v = v.reshape(b, t, h, d)
    lg = jnp.log1p(-jnp.exp2(-5.0 - jnp.arange(h, dtype=F32)))
    c = _chunk_len(t, RET_CHUNK)
    n = t // c
    qc, kc, vc = (a.reshape(b, n, c, h, d) for a in (q, k, v))
    i = jnp.arange(c, dtype=F32)
    diff = i[:, None] - i[None, :]
    dmat = jnp.where(diff >= 0, jnp.exp(jnp.maximum(diff, 0.0)[None] * lg[:, None, None]), 0.0)
    o_in = jnp.einsum('bnhij,bnjhe->bnihe', jnp.einsum('bnihd,bnjhd->bnhij', qc, kc) * dmat, vc)
    q_dec = qc * jnp.exp((i + 1.0)[:, None] * lg)[:, :, None]
    k_dec = kc * jnp.exp((c - 1.0 - i)[:, None] * lg)[:, :, None]
    kv = jnp.einsum('bnjhd,bnjhe->bnhde', k_dec, vc)
    chunk_decay = jnp.exp(c * lg)[:, None, None]

    def step(s, xs):
        qd_i, kv_i = xs
        o = jnp.einsum('bihd,bhde->bihe', qd_i, s)
        return s * chunk_decay + kv_i, o

    s, o_x = lax.scan(step, s0, (jnp.moveaxis(q_dec, 1, 0), jnp.moveaxis(kv, 1, 0)))
    o = (o_in + jnp.moveaxis(o_x, 0, 1)).reshape(b, t, h, d)
    oc = o - jnp.mean(o, axis=-1, keepdims=True)
    o = oc * lax.rsqrt(jnp.mean(oc * oc, axis=-1, keepdims=True) + EPS)
    return jax.nn.silu(gate) * (o.reshape(b, t, h * d) * gn_g), s


_NT = (((1,), (1,)), ((), ()))
GDN_PREP_ROWS = 256
RET_KERNEL_CHUNK = 128


def _iota(shape, dim):
    return lax.broadcasted_iota(jnp.int32, shape, dim)


def _col_block(off, width):
    assert off % width == 0, (off, width)
    return off // width


def _gdn_prep_body(x_ref, halo_ref, cb_ref, w_ref, y_ref, xc_ref, *, tb, nh):
    xc_ref[0:8, :] = jnp.where(pl.program_id(1) == 0, cb_ref[0], halo_ref[0])
    xc_ref[8:, :] = x_ref[0]
    for col in range(3 * nh):
        sl = slice(col * HEAD_DIM, (col + 1) * HEAD_DIM)
        acc = xc_ref[8:, sl] * w_ref[GDN_CONV - 1:GDN_CONV, sl]
        for j in range(GDN_CONV - 1):
            acc = acc + xc_ref[pl.ds(8 - (GDN_CONV - 1) + j, tb), sl] * w_ref[j:j + 1, sl]
        y = acc * jax.nn.sigmoid(acc)
        if col < 2 * nh:
            y = y * lax.rsqrt(jnp.sum(y * y, axis=-1, keepdims=True) + EPS)
            if col < nh:
                y = y * HEAD_DIM ** -0.5
        y_ref[0, :, sl] = y


def _gdn_prep(h3, conv_buf, conv_w, lay):
    b, t, _ = h3.shape
    w = 3 * lay.mix
    tb = _pick_tile(t, GDN_PREP_ROWS, 8)
    cb = jnp.pad(conv_buf, ((0, 0), (8 - conv_buf.shape[1], 0), (0, 0)))
    return pl.pallas_call(
        functools.partial(_gdn_prep_body, tb=tb, nh=lay.nh),
        grid=(b, t // tb),
        in_specs=[pl.BlockSpec((1, tb, w), lambda i, n: (i, n, _col_block(lay.off[0], w))),
                  pl.BlockSpec((1, 8, w), lambda i, n: (i, jnp.maximum(n * (tb // 8) - 1, 0), _col_block(lay.off[0], w))),
                  pl.BlockSpec((1, 8, w), lambda i, n: (i, 0, 0)),
                  pl.BlockSpec((GDN_CONV, w), lambda i, n: (0, 0))],
        out_specs=pl.BlockSpec((1, tb, w), lambda i, n: (i, n, 0)),
        out_shape=jax.ShapeDtypeStruct((b, t, w), F32),
        scratch_shapes=[pltpu.VMEM((tb + 8, w), F32)],
        compiler_params=pltpu.CompilerParams(dimension_semantics=("parallel", "parallel"),
                                             vmem_limit_bytes=VMEM_LIMIT),
        name="gdn_prep",
    )(h3, h3, cb, conv_w)


def _split3(x):
    hi = x.astype(BF16)
    rem = x - hi.astype(F32)
    mid = rem.astype(BF16)
    return hi, mid, (rem - mid.astype(F32)).astype(BF16)


def _mm_hi(x, y):
    xh, xl, _ = _split3(x)
    yh, yl, _ = _split3(y)
    dot = lambda a, b: jnp.dot(a, b, preferred_element_type=F32)
    return dot(xh, yh) + dot(xh, yl) + dot(xl, yh)


def _gdn_body(q_ref, k_ref, v_ref, z_ref, sm_ref, par_ref, ng_ref, s0_ref, o_ref, sout_ref, s_ref, *, c, nh, n_chunks):
    n = pl.program_id(1)

    @pl.when(n == 0)
    def _():
        s_ref[...] = s0_ref[0]

    sm = sm_ref[0]
    x = sm + par_ref[1:2, :]
    softplus = jnp.maximum(x, 0.0) + jnp.log(1.0 + jnp.exp(-jnp.abs(x)))
    gc = -jnp.exp(par_ref[0:1, :]) * softplus
    beta_all = jax.nn.sigmoid(sm)
    row = _iota((c, LANE), 0)
    step = 1
    while step < c:
        gc = gc + jnp.where(row >= step, pltpu.roll(gc, step, 0), 0.0)
        step *= 2
    gct = gc.T
    ii, jj = _iota((c, c), 0), _iota((c, c), 1)
    incl, strict = ii >= jj, ii > jj
    eye = jnp.where(ii == jj, 1.0, 0.0)
    dot = lambda a, b: jnp.dot(a, b, preferred_element_type=F32)
    heads = range(nh)
    sls = [slice(h * HEAD_DIM, (h + 1) * HEAD_DIM) for h in heads]
    gcol = [gc[:, nh + h:nh + h + 1] for h in heads]
    glast = [gc[c - 1:c, nh + h:nh + h + 1] for h in heads]
    beta = [beta_all[:, h:h + 1] for h in heads]
    a, qk, rhs, lhs_q, kd_t = [], [], [], [], []
    for h in heads:
        decay = jnp.where(incl, jnp.exp(jnp.where(incl, gcol[h] - gct[nh + h:nh + h + 1, :], 0.0)), 0.0)
        q, k, v = q_ref[0, :, sls[h]], k_ref[0, :, sls[h]], v_ref[0, :, sls[h]]
        kb = k * beta[h]
        eg = jnp.exp(gcol[h])
        kk = lax.dot_general(jnp.concatenate([kb, q], axis=0).astype(BF16), k.astype(BF16), _NT,
                             preferred_element_type=F32)
        a.append(jnp.where(strict, kk[:c] * decay, 0.0))
        qk.append((kk[c:] * decay).astype(BF16))
        rhs.append(jnp.concatenate([v * beta[h], kb * eg], axis=1).astype(BF16))
        lhs_q.append((q * eg).astype(BF16))
        kd_t.append((k * jnp.exp(glast[h] - gcol[h])).T.astype(BF16))
    t_inv, p = [eye - a[h] for h in heads], a
    power = 2
    while power < c:
        p = [_mm_hi(p[h], p[h]) for h in heads]
        t_inv = [t_inv[h] + _mm_hi(t_inv[h], p[h]) for h in heads]
        power *= 2
    uw = [dot(t_inv[h].astype(BF16), rhs[h]) for h in heads]
    ws = [dot(jnp.concatenate([uw[h][:, HEAD_DIM:].astype(BF16), lhs_q[h]], axis=0), s_ref[h].astype(BF16))
          for h in heads]
    v_new = [(uw[h][:, :HEAD_DIM] - ws[h][:c]).astype(BF16) for h in heads]
    for h in heads:
        s_ref[h] = s_ref[h] * jnp.exp(glast[h]) + dot(kd_t[h], v_new[h])
    for h in heads:
        o = ws[h][c:] + dot(qk[h], v_new[h])
        o = o * lax.rsqrt(jnp.mean(o * o, axis=-1, keepdims=True) + EPS) * ng_ref[...]
        z = z_ref[0, :, sls[h]]
        o_ref[0, :, sls[h]] = (o * (z * jax.nn.sigmoid(z))).astype(o_ref.dtype)

    @pl.when(n == n_chunks - 1)
    def _():
        sout_ref[0] = s_ref[...]


def _gdn_chunks(y, h3, s0, a_log, dt_bias, norm_g, lay):
    b, t, _ = y.shape
    nh, mix = lay.nh, lay.mix
    c = GDN_CHUNK
    n_chunks = t // c
    par = jnp.zeros((2, LANE), F32).at[0, nh:2 * nh].set(a_log).at[1, nh:2 * nh].set(dt_bias)
    assert lay.off[3] - lay.off[2] == nh
    tok = lambda off, width: pl.BlockSpec((1, c, width), lambda i, n: (i, n, _col_block(off, width)))
    state = pl.BlockSpec((1, nh, HEAD_DIM, HEAD_DIM), lambda i, n: (i, 0, 0, 0))
    return pl.pallas_call(
        functools.partial(_gdn_body, c=c, nh=nh, n_chunks=n_chunks),
        grid=(b, n_chunks),
        in_specs=[tok(0, mix), tok(mix, mix), tok(2 * mix, mix), tok(lay.off[1], mix), tok(lay.off[2], LANE),
                  pl.BlockSpec((2, LANE), lambda i, n: (0, 0)), pl.BlockSpec((1, HEAD_DIM), lambda i, n: (0, 0)),
                  state],
        out_specs=[pl.BlockSpec((1, c, mix), lambda i, n: (i, n, 0)), state],
        out_shape=[jax.ShapeDtypeStruct((b, t, mix), BF16), jax.ShapeDtypeStruct(s0.shape, F32)],
        scratch_shapes=[pltpu.VMEM((nh, HEAD_DIM, HEAD_DIM), F32)],
        compiler_params=pltpu.CompilerParams(dimension_semantics=("parallel", "arbitrary"),
                                             vmem_limit_bytes=VMEM_LIMIT),
        name="gdn_chunks",
    )(y, y, y, h3, h3, par, norm_g.reshape(1, HEAD_DIM), s0)


def _ret_body(q_ref, k_ref, v_ref, g_ref, cos_ref, sin_ref, gn_ref, s0_ref, o_ref, sout_ref, s_ref, *, c, nh, n_chunks):
    n = pl.program_id(1)

    @pl.when(n == 0)
    def _():
        s_ref[...] = s0_ref[0]

    cos, sin = cos_ref[...], sin_ref[...]
    ii, jj = _iota((c, c), 0), _iota((c, c), 1)
    lower = ii >= jj
    dist = jnp.where(lower, ii - jj, 0).astype(F32)
    icol = _iota((c, 1), 0).astype(F32)
    dot = lambda a, b: jnp.dot(a, b, preferred_element_type=F32)
    rope = lambda x: x * cos + pltpu.roll(x, HEAD_DIM // 2, 1) * sin
    for h in range(nh):
        lg = math.log1p(-2.0 ** (-5.0 - h))
        sl = slice(h * HEAD_DIM, (h + 1) * HEAD_DIM)
        q = rope(q_ref[0, :, sl])
        k = rope(k_ref[0, :, sl]) * HEAD_DIM ** -0.5
        v_bf = v_ref[0, :, sl].astype(BF16)
        qk = lax.dot_general(q.astype(BF16), k.astype(BF16), _NT, preferred_element_type=F32)
        qk = qk * jnp.where(lower, jnp.exp(dist * lg), 0.0)
        s_bf = s_ref[h].astype(BF16)
        o = dot(qk.astype(BF16), v_bf) + dot((q * jnp.exp((icol + 1.0) * lg)).astype(BF16), s_bf)
        k_dec = k * jnp.exp((c - 1.0 - icol) * lg)
        s_ref[h] = s_ref[h] * math.exp(c * lg) + dot(k_dec.T.astype(BF16), v_bf)
        oc = o - jnp.mean(o, axis=-1, keepdims=True)
        o = oc * lax.rsqrt(jnp.mean(oc * oc, axis=-1, keepdims=True) + EPS)
        gate = g_ref[0, :, sl]
        o_ref[0, :, sl] = (gate * jax.nn.sigmoid(gate) * (o * gn_ref[:, sl])).astype(o_ref.dtype)

    @pl.when(n == n_chunks - 1)
    def _():
        sout_ref[0] = s_ref[...]


def _ret_chunks(h3, pos, s0, gn_g, lay):
    b, t, _ = h3.shape
    nh, mix = lay.nh, lay.mix
    c = _pick_tile(t, RET_KERNEL_CHUNK, 8)
    n_chunks = t // c
    half = HEAD_DIM // 2
    freq = jnp.exp(-math.log(ROPE_BASE) * jnp.arange(half, dtype=F32) / half)
    ang = pos.astype(F32)[:, None] * freq
    cos, sin = jnp.cos(ang), jnp.sin(ang)
    cos2, sin2 = jnp.concatenate([cos, cos], axis=-1), jnp.concatenate([-sin, sin], axis=-1)
    tok = lambda off: pl.BlockSpec((1, c, mix), lambda i, n: (i, n, _col_block(off, mix)))
    table = pl.BlockSpec((c, HEAD_DIM), lambda i, n: (n, 0))
    state = pl.BlockSpec((1, nh, HEAD_DIM, HEAD_DIM), lambda i, n: (i, 0, 0, 0))
    return pl.pallas_call(
        functools.partial(_ret_body, c=c, nh=nh, n_chunks=n_chunks),
        grid=(b, n_chunks),
        in_specs=[tok(lay.off[4]), tok(lay.off[5]), tok(lay.off[6]), tok(lay.off[7]), table, table,
                  pl.BlockSpec((1, mix), lambda i, n: (0, 0)), state],
        out_specs=[pl.BlockSpec((1, c, mix), lambda i, n: (i, n, 0)), state],
        out_shape=[jax.ShapeDtypeStruct((b, t, mix), BF16), jax.ShapeDtypeStruct(s0.shape, F32)],
        scratch_shapes=[pltpu.VMEM((nh, HEAD_DIM, HEAD_DIM), F32)],
        compiler_params=pltpu.CompilerParams(dimension_semantics=("parallel", "arbitrary"),
                                             vmem_limit_bytes=VMEM_LIMIT),
        name="retention_chunks",
    )(h3, h3, h3, h3, cos2, sin2, gn_g.reshape(1, mix), s0)


INT_MIN = -2 ** 31
PAD_SCORE = -3e38


def _sortable(x):
    b = pltpu.bitcast(x, jnp.int32)
    return b ^ ((b >> 31) & 0x7FFFFFFF)


def _kth_largest(key_ref, nc, k, n_live):
    def count_ge(t):
        if n_live is None:
            acc = jnp.where(key_ref[0] >= t, 1.0, 0.0)
            for c in range(1, nc):
                acc = acc + jnp.where(key_ref[c] >= t, 1.0, 0.0)
        else:
            acc = lax.fori_loop(0, n_live, lambda c, acc: acc + jnp.where(key_ref[c] >= t, 1.0, 0.0),
                                jnp.zeros(key_ref.shape[1:], F32))
        return jnp.sum(acc, axis=-1, keepdims=True)

    tq = key_ref.shape[1]
    zero = jnp.zeros((tq, 1), jnp.int32)
    base = jnp.where(count_ge(zero) >= k, zero, jnp.full((tq, 1), INT_MIN, jnp.int32))

    def body(i, base):
        cand = base | (jnp.int32(1) << (30 - i))
        return jnp.where(count_ge(cand) >= k, cand, base)

    return lax.fori_loop(0, 31, body, base)


def _topk_blocks(key_ref, nc, k, n_live=None):
    _, tq, lc = key_ref.shape
    thr = _kth_largest(key_ref, nc, k, n_live)
    n_gt = jnp.where(key_ref[0] > thr, 1.0, 0.0)
    for c in range(1, nc):
        n_gt = n_gt + jnp.where(key_ref[c] > thr, 1.0, 0.0)
    need = k - jnp.sum(n_gt, axis=-1, keepdims=True)
    tri = jnp.where(_iota((LANE, LANE), 0) <= _iota((LANE, LANE), 1), 1.0, 0.0).astype(BF16)
    carry = jnp.zeros((tq, 1), F32)
    for c in range(nc):
        for j in range(lc // LANE):
            kk = key_ref[c, :, j * LANE:(j + 1) * LANE]
            eq = kk == thr
            eqf = jnp.where(eq, 1.0, 0.0)
            prefix = jnp.dot(eqf.astype(BF16), tri, preferred_element_type=F32) + carry
            carry = carry + jnp.sum(eqf, axis=-1, keepdims=True)
            yield c, j, (kk > thr) | (eq & (prefix <= need))


def _dsa_select_body(qi_ref, sm_ref, ki_ref, o_ref, qh32_ref, qh_ref, d_ref, key_ref, *, tq, lc, nc, pos0, topk, wi_off):
    q_first = pos0 + pl.program_id(1) * tq
    qpos = q_first + _iota((tq, 1), 0)
    n_live = jnp.minimum(nc, (q_first + tq - 1) // lc + 1)
    qi = qi_ref[0]
    for h in range(IDX_HEADS):
        qh32_ref[h * tq:(h + 1) * tq, :] = qi[:, h * IDX_DIM:(h + 1) * IDX_DIM]
    qh_ref[...] = qh32_ref[...].astype(BF16)
    sm = sm_ref[0]
    for c in range(nc):
        @pl.when(c < n_live)
        def _():
            kic = ki_ref[0, c * lc:(c + 1) * lc, :IDX_DIM].astype(BF16)
            d_ref[...] = lax.dot_general(qh_ref[...], kic, _NT, preferred_element_type=F32)
            score = jnp.zeros((tq, lc), F32)
            for h in range(IDX_HEADS):
                score = score + sm[:, wi_off + h:wi_off + h + 1] * jnp.maximum(d_ref[h * tq:(h + 1) * tq, :], 0.0)
            kpos = c * lc + _iota((tq, lc), 1)
            key_ref[c] = _sortable(jnp.where(kpos <= qpos, score, NEG))

        @pl.when(c >= n_live)
        def _():
            key_ref[c] = _sortable(jnp.full((tq, lc), NEG, F32))
    pieces = []
    for c, j, sel in _topk_blocks(key_ref, nc, topk, n_live):
        kpos = c * lc + j * LANE + _iota((tq, LANE), 1)
        pieces.append(jnp.where(sel & (kpos <= qpos), 1.0, 0.0))
        if len(pieces) == lc // LANE:
            o_ref[0, c] = jnp.concatenate(pieces, axis=1).astype(BF16)
            pieces = []


def _dsa_select(hq, qi_off, sm_off, rows, ki_off, *, tq, lc, pos0, l_real, wi_off):
    b, t, _ = hq.shape
    l = rows.shape[1]
    nc = l // lc
    topk = min(DSA_TOPK_MAX, l_real // 4)
    qw = IDX_HEADS * IDX_DIM
    return pl.pallas_call(
        functools.partial(_dsa_select_body, tq=tq, lc=lc, nc=nc, pos0=pos0, topk=topk, wi_off=wi_off),
        grid=(b, t // tq),
        in_specs=[pl.BlockSpec((1, tq, qw), lambda i, q: (i, q, _col_block(qi_off, qw))),
                  pl.BlockSpec((1, tq, LANE), lambda i, q: (i, q, _col_block(sm_off, LANE))),
                  pl.BlockSpec((1, l, LANE), lambda i, q: (i, 0, _col_block(ki_off, LANE)))],
        out_specs=pl.BlockSpec((1, nc, tq, lc), lambda i, q: (i, 0, q, 0)),
        out_shape=jax.ShapeDtypeStruct((b, nc, t, lc), BF16),
        scratch_shapes=[pltpu.VMEM((IDX_HEADS * tq, IDX_DIM), F32), pltpu.VMEM((IDX_HEADS * tq, IDX_DIM), BF16),
                        pltpu.VMEM((IDX_HEADS * tq, lc), F32), pltpu.VMEM((nc, tq, lc), jnp.int32)],
        compiler_params=pltpu.CompilerParams(dimension_semantics=("parallel", "arbitrary"),
                                             vmem_limit_bytes=VMEM_LIMIT),
        name="dsa_select",
    )(hq, hq, rows)


def _nsa_select_body(q_ref, kc0_ref, kc1_ref, vc0_ref, vc1_ref, wk_ref, wv_ref, bk_ref, bv_ref, ov_ref, e_ref,
                     ocmp_ref, mask_ref, ck_ref, cv_ref, cacc_ref, key_ref, *, tq, lc, nc, pos0, l16, ns, n_sel, grp):
    qb = pl.program_id(1)

    @pl.when(qb == 0)
    def _():
        for srcs, w_ref, b_ref, dst in (((kc0_ref, kc1_ref), wk_ref, bk_ref, ck_ref),
                                        ((vc0_ref, vc1_ref), wv_ref, bv_ref, cv_ref)):
            for g in range(KV_HEADS):
                for c in range(CMP_STRIDE):
                    x = srcs[g][0, pl.ds(c, l16, stride=CMP_STRIDE), :].astype(BF16)
                    part = jnp.dot(x, w_ref[c], preferred_element_type=F32)
                    if c == 0:
                        cacc_ref[...] = part
                    else:
                        cacc_ref[...] += part
                acc = cacc_ref[...]
                summ = acc[:, :HEAD_DIM] + pltpu.roll(acc[:, HEAD_DIM:], l16 - 1, 0) + b_ref[...]
                dst[g] = summ.astype(BF16)

    qpos = pos0 + qb * tq + _iota((tq, 1), 0)
    vis = (_iota((tq, l16), 1) * CMP_STRIDE + (CMP_BLOCK - 1)) <= qpos
    jcol = _iota((tq, ns), 1)
    cur = qpos >> int(math.log2(SEL_BLOCK))
    forced = (jcol == 0) | (jcol == cur) | (jcol == cur - 1)
    admissible = jcol * SEL_BLOCK <= qpos
    scale = HEAD_DIM ** -0.5
    for g in range(KV_HEADS):
        imp_c = jnp.zeros((tq, l16), F32)
        for r in range(grp):
            hd = g * grp + r
            q = q_ref[0][:, hd * HEAD_DIM:(hd + 1) * HEAD_DIM].astype(BF16)
            s = lax.dot_general(q, ck_ref[g], _NT, preferred_element_type=F32) * scale
            s = jnp.where(vis, s, NEG)
            e = jnp.exp(s - jnp.max(s, axis=-1, keepdims=True))
            p = jnp.where(vis, e / jnp.sum(e, axis=-1, keepdims=True), 0.0)
            imp_c = imp_c + p
            ocmp_ref[0, :, hd * HEAD_DIM:(hd + 1) * HEAD_DIM] = jnp.dot(
                p.astype(BF16), cv_ref[g], preferred_element_type=F32)
        hi = imp_c.astype(BF16)
        rem = imp_c - hi.astype(F32)
        mid = rem.astype(BF16)
        lo = (rem - mid.astype(F32)).astype(BF16)
        ov = ov_ref[...]
        imp = (jnp.dot(hi, ov, preferred_element_type=F32) + jnp.dot(mid, ov, preferred_element_type=F32)
               + jnp.dot(lo, ov, preferred_element_type=F32))
        imp = jnp.where(admissible, jnp.where(forced, imp + FORCE_BONUS, imp), NEG)
        imp = jnp.where(jcol < n_sel, imp, PAD_SCORE)
        key_ref[0] = _sortable(imp)
        sel = jnp.concatenate([jnp.where(s_, 1.0, 0.0) for _, _, s_ in _topk_blocks(key_ref, 1, min(SEL_TOPN, n_sel))],
                              axis=1).astype(BF16)
        for c in range(nc):
            tok = jnp.dot(sel, e_ref[:, c * lc:(c + 1) * lc], preferred_element_type=F32)
            kpos = c * lc + _iota((tq, lc), 1)
            mask_ref[g, c] = jnp.where((tok > 0.5) & (kpos <= qpos), 1.0, 0.0).astype(BF16)


def _nsa_select(hq, q_off, rows, kc_off, vc_off, cwk, cwv, pek, pev, *, tq, lc, pos0, l_real, nh):
    b, t, _ = hq.shape
    l = rows.shape[1]
    nc = l // lc
    grp = nh // KV_HEADS
    n_cmp = (l_real - CMP_BLOCK) // CMP_STRIDE + 1
    l16 = n_cmp + 1
    n_sel = -(-l_real // SEL_BLOCK)
    ns = _round_up(n_sel, LANE)
    pack = lambda w: jnp.concatenate([w[:CMP_STRIDE], w[CMP_STRIDE:]], axis=-1).astype(BF16)
    bias = lambda pe, w: jnp.einsum('cd,cde->e', pe, w, precision=lax.Precision.HIGHEST).reshape(1, HEAD_DIM)
    c0 = np.arange(l16)[:, None] * CMP_STRIDE
    s0 = np.arange(ns)[None, :] * SEL_BLOCK
    overlap = jnp.asarray((c0 < s0 + SEL_BLOCK) & (c0 + CMP_BLOCK > s0), BF16)
    expand = jnp.asarray(np.arange(l)[None, :] // SEL_BLOCK == np.arange(ns)[:, None], BF16)
    qw = nh * HEAD_DIM
    lrows = CMP_STRIDE * l16
    const = lambda shape: pl.BlockSpec(shape, lambda i, q: (0,) * len(shape))
    head_rows = lambda off, g: pl.BlockSpec((1, lrows, HEAD_DIM), lambda i, q: (i, 0, _col_block(off, HEAD_DIM) + g))
    assert KV_HEADS == 2
    return pl.pallas_call(
        functools.partial(_nsa_select_body, tq=tq, lc=lc, nc=nc, pos0=pos0, l16=l16, ns=ns, n_sel=n_sel, grp=grp),
        grid=(b, t // tq),
        in_specs=[pl.BlockSpec((1, tq, qw), lambda i, q: (i, q, _col_block(q_off, qw))),
                  head_rows(kc_off, 0), head_rows(kc_off, 1), head_rows(vc_off, 0), head_rows(vc_off, 1),
                  const((CMP_STRIDE, HEAD_DIM, 2 * HEAD_DIM)), const((CMP_STRIDE, HEAD_DIM, 2 * HEAD_DIM)),
                  const((1, HEAD_DIM)), const((1, HEAD_DIM)), const((l16, ns)), const((ns, l))],
        out_specs=[pl.BlockSpec((1, tq, qw), lambda i, q: (i, q, 0)),
                   pl.BlockSpec((KV_HEADS, nc, tq, lc), lambda i, q: (i, 0, q, 0))],
        out_shape=[jax.ShapeDtypeStruct((b, t, qw), F32),
                   jax.ShapeDtypeStruct((b * KV_HEADS, nc, t, lc), BF16)],
        scratch_shapes=[pltpu.VMEM((KV_HEADS, l16, HEAD_DIM), BF16), pltpu.VMEM((KV_HEADS, l16, HEAD_DIM), BF16),
                        pltpu.VMEM((l16, 2 * HEAD_DIM), F32), pltpu.VMEM((1, tq, ns), jnp.int32)],
        compiler_params=pltpu.CompilerParams(dimension_semantics=("parallel", "arbitrary"),
                                             vmem_limit_bytes=VMEM_LIMIT),
        name="nsa_select",
    )(hq, rows, rows, rows, rows, pack(cwk), pack(cwv), bias(pek, cwk), bias(pev, cwv), overlap, expand)


def _attend_body(*refs, banded, tq, lc, nc, pos0, kpos0, grp):
    if banded:
        q_ref, k_ref, v_ref, o_ref, s_ref, acc_ref = refs
    else:
        q_ref, k_ref, v_ref, m_ref, o_ref, s_ref, acc_ref = refs
    q_first = pos0 + pl.program_id(2) * tq
    qpos = q_first + _iota((tq, 1), 0)
    c_hi = jnp.minimum(nc, (q_first + tq - 1 - kpos0) // lc + 1)
    c_lo = jnp.maximum(q_first - WINDOW - kpos0, 0) // lc if banded else 0
    scale = HEAD_DIM ** -0.5
    qs = [q_ref[0][:, r * HEAD_DIM:(r + 1) * HEAD_DIM].astype(BF16) for r in range(grp)]

    def scores(c, ms):
        kc = k_ref[0, pl.ds(pl.multiple_of(c * lc, lc), lc), :].astype(BF16)
        if banded:
            kpos = kpos0 + c * lc + _iota((tq, lc), 1)
            valid = (kpos <= qpos) & (qpos - kpos <= WINDOW)
        else:
            valid = m_ref[0, c].astype(F32) > 0.5
        out = []
        for r in range(grp):
            s = jnp.where(valid, lax.dot_general(qs[r], kc, _NT, preferred_element_type=F32) * scale, NEG)
            s_ref[r, c] = s
            out.append(jnp.maximum(ms[r], jnp.max(s, axis=-1, keepdims=True)))
        return tuple(out)

    ms = lax.fori_loop(c_lo, c_hi, scores, tuple(jnp.full((tq, 1), NEG, F32) for _ in range(grp)))
    acc_ref[...] = jnp.zeros_like(acc_ref)

    def values(c, ls):
        vc = v_ref[0, pl.ds(pl.multiple_of(c * lc, lc), lc), :].astype(BF16)
        out = []
        for r in range(grp):
            p = jnp.exp(s_ref[r, c] - ms[r])
            acc_ref[r] += jnp.dot(p.astype(BF16), vc, preferred_element_type=F32)
            out.append(ls[r] + jnp.sum(p, axis=-1, keepdims=True))
        return tuple(out)

    ls = lax.fori_loop(c_lo, c_hi, values, tuple(jnp.zeros((tq, 1), F32) for _ in range(grp)))
    for r in range(grp):
        o_ref[0, :, r * HEAD_DIM:(r + 1) * HEAD_DIM] = (acc_ref[r] / ls[r]).astype(o_ref.dtype)


def _attend(hq, q_off, kv, k_off, v_off, mask, *, tq, lc, pos0, kpos0, nh, mask_per_group=False, out_dtype=F32,
            name="attend"):
    b, t, _ = hq.shape
    l = kv.shape[1]
    nc = l // lc
    grp = nh // KV_HEADS
    qw = grp * HEAD_DIM
    banded = mask is None
    in_specs = [pl.BlockSpec((1, tq, qw), lambda i, g, q: (i, q, _col_block(q_off, qw) + g)),
                pl.BlockSpec((1, l, HEAD_DIM), lambda i, g, q: (i, 0, _col_block(k_off, HEAD_DIM) + g)),
                pl.BlockSpec((1, l, HEAD_DIM), lambda i, g, q: (i, 0, _col_block(v_off, HEAD_DIM) + g))]
    args = [hq, kv, kv]
    if not banded:
        if mask_per_group:
            in_specs.append(pl.BlockSpec((1, nc, tq, lc), lambda i, g, q: (i * KV_HEADS + g, 0, q, 0)))
        else:
            in_specs.append(pl.BlockSpec((1, nc, tq, lc), lambda i, g, q: (i, 0, q, 0)))
        args.append(mask)
    return pl.pallas_call(
        functools.partial(_attend_body, banded=banded, tq=tq, lc=lc, nc=nc, pos0=pos0, kpos0=kpos0, grp=grp),
        grid=(b, KV_HEADS, t // tq),
        in_specs=in_specs,
        out_specs=pl.BlockSpec((1, tq, qw), lambda i, g, q: (i, q, g)),
        out_shape=jax.ShapeDtypeStruct((b, t, nh * HEAD_DIM), out_dtype),
        scratch_shapes=[pltpu.VMEM((grp, nc, tq, lc), F32), pltpu.VMEM((grp, tq, HEAD_DIM), F32)],
        compiler_params=pltpu.CompilerParams(dimension_semantics=("parallel", "parallel", "arbitrary"),
                                             vmem_limit_bytes=VMEM_LIMIT),
        name=name,
    )(*args)


def _nsa_combine_body(a_ref, b_ref, c_ref, sm_ref, o_ref, *, gate_off, nh):
    gate = jax.nn.sigmoid(sm_ref[...])
    for hd in range(nh):
        sl = slice(hd * HEAD_DIM, (hd + 1) * HEAD_DIM)
        col = lambda br: gate[:, gate_off + br * nh + hd:gate_off + br * nh + hd + 1]
        o = col(0) * a_ref[:, sl] + col(1) * b_ref[:, sl] + col(2) * c_ref[:, sl]
        o_ref[:, sl] = o.astype(o_ref.dtype)


def _nsa_combine(o_cmp, o_sel, o_win, h, sm_off, gate_off, nh):
    m, w = o_cmp.shape
    tm = _pick_tile(m, 512, 8)
    blk = pl.BlockSpec((tm, w), lambda i: (i, 0))
    return pl.pallas_call(
        functools.partial(_nsa_combine_body, gate_off=gate_off, nh=nh),
        grid=(m // tm,),
        in_specs=[blk, blk, blk, pl.BlockSpec((tm, LANE), lambda i: (i, _col_block(sm_off, LANE)))],
        out_specs=blk,
        out_shape=jax.ShapeDtypeStruct((m, w), BF16),
        compiler_params=pltpu.CompilerParams(dimension_semantics=("parallel",), vmem_limit_bytes=VMEM_LIMIT),
        name="nsa_combine",
    )(o_cmp, o_sel, o_win, h)


def _sparse_mixers(h, hq, kv, kv_off, win, win_off, win_pos0, lay, cw, *, tq, pos0, l_real, t_real):
    cwk, cwv, pek, pev = cw
    nh = lay.nh
    b = hq.shape[0]
    lc = _pick_tile(kv.shape[1], 32 * 1024 // tq, LANE)
    lcw = _pick_tile(win.shape[1], 16 * 1024 // tq, LANE)
    sm_off = lay.off[2]
    offs = np.concatenate([[0], np.cumsum(PAGED_SIZES)]).tolist()
    common = dict(tq=tq, pos0=pos0, nh=nh)
    o_cmp, sel_mask = _nsa_select(hq, lay.off[8], kv, kv_off + offs[0], kv_off + offs[1], cwk, cwv, pek, pev,
                                  lc=lc, l_real=l_real, **common)
    o_sel = _attend(hq, lay.off[8], kv, kv_off + offs[2], kv_off + offs[3], sel_mask, lc=lc, kpos0=0,
                    mask_per_group=True, name="nsa_selected", **common)
    o_win = _attend(hq, lay.off[8], win, win_off, win_off + KVW, None, lc=lcw, kpos0=win_pos0,
                    name="nsa_window", **common)
    dsa_mask = _dsa_select(hq, lay.off[19], sm_off, kv, kv_off + offs[6], tq=tq, lc=lc, pos0=pos0, l_real=l_real,
                           wi_off=lay.off[21] - sm_off)
    o_d = _attend(hq, lay.off[16], kv, kv_off + offs[4], kv_off + offs[5], dsa_mask, lc=lc, kpos0=0,
                  out_dtype=BF16, name="dsa_attend", **common)
    unpad = lambda o: o[:, :t_real].reshape(b * t_real, nh * HEAD_DIM)
    o_c = _nsa_combine(unpad(o_cmp), unpad(o_sel), unpad(o_win), h, sm_off, lay.off[15] - sm_off, nh)
    return o_c, unpad(o_d)


def _layer(x, q_pos, lw, past, lay):
    (g_mix, g_mlp, w_in_p, conv_w, a_log, dt_bias, gdn_g, ret_g, cwk, cwv, pek, pev, w_br, w_o, w_up_l, w_down_l) = lw
    conv_buf, s_gdn, s_ret, win_buf, past_rows = past
    b, t, d = x.shape
    m = b * t
    mix, nh = lay.mix, lay.nh
    x2d = x.reshape(m, d)
    h = _matmul(_rmsnorm(x2d, g_mix, BF16), w_in_p, name="in_proj")
    h3 = h.reshape(b, t, lay.width)
    seg = lambda idx: h3[:, :, lay.off[idx]:lay.off[idx] + lay.in_sizes[idx]]

    if t % GDN_CHUNK == 0:
        o_a, s_gdn_new = _gdn_chunks(_gdn_prep(h3, conv_buf, conv_w, lay), h3, s_gdn, a_log, dt_bias, gdn_g, lay)
        conv_new = jnp.concatenate([conv_buf, seg(0)[:, t - (GDN_CONV - 1):]], axis=1)[:, -(GDN_CONV - 1):]
        o_b, s_ret_new = _ret_chunks(h3, q_pos, s_ret, ret_g, lay)
    else:
        o_a, conv_new, s_gdn_new = _gdn(seg(0), seg(1), seg(2), seg(3), conv_buf, s_gdn, conv_w, a_log, dt_bias,
                                        gdn_g, nh)
        o_b, s_ret_new = _retention(seg(4), seg(5), seg(6), seg(7), q_pos, s_ret, ret_g, nh)

    new_rows = h3[:, :, lay.off[9]:lay.off[9] + PAGED_W]
    win_rows = h3[:, :, lay.off[13]:lay.off[13] + 2 * KVW]
    cw = (cwk, cwv, pek, pev)
    if past_rows is None:
        win_new = win_rows[:, t - min(WINDOW, t):]
        o_c, o_d = _sparse_mixers(h, h3, h3, lay.off[9], h3, lay.off[13], 0, lay, cw,
                                  tq=Q_BLOCK, pos0=0, l_real=t, t_real=t)
    else:
        past_len, wb = past_rows.shape[1], win_buf.shape[1]
        tq = _round_up(t, 8)
        pad_rows = lambda a, n: jnp.pad(a, ((0, 0), (0, n - a.shape[1]), (0, 0)))
        l_real = past_len + t
        lane_pad = _round_up(PAGED_W, LANE) - PAGED_W
        kv = jnp.pad(past_rows, ((0, 0), (0, _round_up(l_real, 2 * LANE) - past_len), (0, lane_pad)))
        kv = lax.dynamic_update_slice(kv, jnp.pad(new_rows, ((0, 0), (0, 0), (0, lane_pad))), (0, past_len, 0))
        win_all = jnp.concatenate([win_buf, win_rows], axis=1)
        win_new = win_all[:, t:]
        o_c, o_d = _sparse_mixers(h, pad_rows(h3, tq), kv, 0, pad_rows(win_all, _round_up(wb + t, LANE)), 0,
                                  past_len - wb, lay, cw, tq=tq, pos0=past_len, l_real=l_real, t_real=t)

    o_stack = jnp.stack([o.reshape(m, mix).astype(BF16) for o in (o_a, o_b, o_c, o_d)])
    mixed = _merge_branches(o_stack, w_br, h, lay)
    x2 = _matmul(mixed, w_o, epilogue="residual", res=x2d, name="out_proj")
    up = _matmul(_rmsnorm(x2, g_mlp, BF16), w_up_l, epilogue="relu2", out_dtype=BF16, name="ffn_up")
    x3 = _matmul(up, w_down_l, epilogue="residual", res=x2, name="ffn_down")
    return x3.reshape(b, t, d), (new_rows, conv_new, s_gdn_new, s_ret_new, win_new)


def kernel(x_prompt, x_sample, cache_kv, cache_nsa_window, state_gdn, state_gdn_conv, state_retention, page_table, norm_mix, norm_mlp, norm_final, w_in, gdn_conv_w, gdn_a_log, gdn_dt_bias, gdn_norm, ret_norm, nsa_cmp_wk, nsa_cmp_wv, nsa_cmp_pe_k, nsa_cmp_pe_v, w_branch, w_out, w_up, w_down):
    bp, tp, d = x_prompt.shape
    bs, ts, _ = x_sample.shape
    depth = w_in.shape[0]
    lay = _Layout(d)
    past_len = page_table.shape[1] * cache_kv.shape[1]
    pos_p = jnp.arange(tp, dtype=jnp.int32)
    pos_s = past_len + jnp.arange(ts, dtype=jnp.int32)
    zero_conv = jnp.zeros((bp, GDN_CONV - 1, 3 * lay.mix), F32)
    zero_state = jnp.zeros((bp, lay.nh, HEAD_DIM, HEAD_DIM), F32)
    w_in_p = _pack_w_in(w_in, lay)
    w_br_b, w_o_b, w_up_b, w_down_b = (a.astype(BF16) for a in (w_branch, w_out, w_up, w_down))
    hp, hs = x_prompt, x_sample
    out_p, out_s = [], []
    for l in range(depth):
        lw = (norm_mix[l], norm_mlp[l], w_in_p[l], gdn_conv_w[l], gdn_a_log[l], gdn_dt_bias[l], gdn_norm[l],
              ret_norm[l], nsa_cmp_wk[l], nsa_cmp_wv[l], nsa_cmp_pe_k[l], nsa_cmp_pe_v[l], w_br_b[l],
              w_o_b[l], w_up_b[l], w_down_b[l])
        hp, st_p = _layer(hp, pos_p, lw, (zero_conv, zero_state, zero_state, None, None), lay)
        out_p.append(st_p)
        past_rows = cache_kv[page_table, :, l].reshape(bs, past_len, PAGED_W)
        past_s = (state_gdn_conv[:, l], state_gdn[:, l], state_retention[:, l], cache_nsa_window[:, l], past_rows)
        hs, st_s = _layer(hs, pos_s, lw, past_s, lay)
        out_s.append(st_s)
    y_prompt = _rmsnorm(hp.reshape(bp * tp, d), norm_final, F32).reshape(bp, tp, d)
    y_sample = _rmsnorm(hs.reshape(bs * ts, d), norm_final, F32).reshape(bs, ts, d)
    stack = lambda outs, j, axis: jnp.stack([o[j] for o in outs], axis=axis)
    return (y_prompt, y_sample,
            stack(out_p, 0, 2), stack(out_s, 0, 2),
            stack(out_p, 2, 1), stack(out_s, 2, 1),
            stack(out_p, 1, 1), stack(out_s, 1, 1),
            stack(out_p, 3, 1), stack(out_s, 3, 1),
            stack(out_p, 4, 1), stack(out_s, 4, 1))
```

```python
import functools
import math

import jax
import jax.numpy as jnp
import numpy as np
from jax import lax
from jax.experimental import pallas as pl
from jax.experimental.pallas import tpu as pltpu

HEAD_DIM = 128
N_BRANCH = 4
EPS = 1e-6
NEG = -1e30
GDN_CONV = 4
GDN_CHUNK = 64
RET_CHUNK = 64
ROPE_BASE = 10000.0
KV_HEADS = 2
CMP_BLOCK = 32
CMP_STRIDE = 16
SEL_BLOCK = 64
SEL_TOPN = 16
WINDOW = 512
FORCE_BONUS = 1e4
IDX_HEADS = 16
IDX_DIM = 64
DSA_TOPK_MAX = 256
Q_BLOCK = 128
KVW = KV_HEADS * HEAD_DIM
PAGED_SIZES = (KVW, KVW, KVW, KVW, KVW, KVW, IDX_DIM)
PAGED_W = sum(PAGED_SIZES)

LANE = 128
MXU_DIM = 256
VMEM_LIMIT = 56 * 1024 * 1024

F32 = jnp.float32
BF16 = jnp.bfloat16


def _round_up(n, m):
    return -(-n // m) * m


def _pick_tile(dim, target, align):
    best = None
    for t in range(align, min(dim, target) + 1, align):
        if dim % t == 0:
            best = t
    return best if best is not None else dim


class _Layout:
    def __init__(self, d_model):
        mix = d_model // N_BRANCH
        nh = mix // HEAD_DIM
        self.d_model, self.mix, self.nh = d_model, mix, nh
        self.in_sizes = (
            3 * mix, mix, nh, nh,
            mix, mix, mix, mix,
            mix, KVW, KVW, KVW, KVW, KVW, KVW, 3 * nh,
            mix, KVW, KVW, IDX_HEADS * IDX_DIM, IDX_DIM, IDX_HEADS,
            N_BRANCH * d_model)
        self.order = [0, 1, 4, 5, 6, 7, 8, 16, 19, 22, 9, 10, 11, 12, 17, 18, 20, None, 13, 14, 2, 3, 15, 21, None]
        self.off = {}
        pos = 0
        self.pads = []
        for idx in self.order:
            if idx is None:
                pad = _round_up(pos, LANE) - pos
                self.pads.append(pad)
                pos += pad
            else:
                self.off[idx] = pos
                pos += self.in_sizes[idx]
        self.width = pos


SUBLANE = 8


def _pack_body(x_ref, o_ref, *, tiles):
    for t, pieces in enumerate(tiles):
        rows = [jnp.zeros((n, LANE), F32) if s is None else x_ref[0, s:s + n, :] for s, n in pieces]
        tile = rows[0] if len(rows) == 1 else jnp.concatenate(rows, axis=0)
        o_ref[0, :, t * LANE:(t + 1) * LANE] = tile.T.astype(BF16)


def _pack_w_in(w_in, lay):
    depth, d, in_width = w_in.shape
    starts = np.concatenate([[0], np.cumsum(lay.in_sizes)]).tolist()
    tiles, pads = [[]], iter(lay.pads)
    for idx in lay.order:
        src, width = (None, next(pads)) if idx is None else (starts[idx], lay.in_sizes[idx])
        while width:
            room = LANE - sum(n for _, n in tiles[-1])
            if room == 0:
                tiles.append([])
                room = LANE
            n = min(room, width)
            assert n % SUBLANE == 0 and (src is None or src % SUBLANE == 0)
            if tiles[-1] and src is not None and tiles[-1][-1][0] is not None \
                    and tiles[-1][-1][0] + tiles[-1][-1][1] == src:
                tiles[-1][-1] = (tiles[-1][-1][0], tiles[-1][-1][1] + n)
            else:
                tiles[-1].append((src, n))
            src = None if src is None else src + n
            width -= n
    assert len(tiles) * LANE == lay.width
    w_t = jnp.swapaxes(w_in, 1, 2)
    return pl.pallas_call(
        functools.partial(_pack_body, tiles=tiles),
        grid=(depth, d // LANE),
        in_specs=[pl.BlockSpec((1, in_width, LANE), lambda l, i: (l, 0, i))],
        out_specs=pl.BlockSpec((1, LANE, lay.width), lambda l, i: (l, i, 0)),
        out_shape=jax.ShapeDtypeStruct((depth, d, lay.width), BF16),
        compiler_params=pltpu.CompilerParams(dimension_semantics=("parallel", "parallel"),
                                             vmem_limit_bytes=VMEM_LIMIT),
        name="pack_w_in",
    )(w_t)


def _rmsnorm_body(x_ref, g_ref, o_ref):
    x = x_ref[...]
    y = x * lax.rsqrt(jnp.mean(x * x, axis=-1, keepdims=True) + EPS)
    o_ref[...] = (y * g_ref[...]).astype(o_ref.dtype)


def _rmsnorm(x, g, out_dtype):
    m, d = x.shape
    tm = _pick_tile(m, 256, 8)
    return pl.pallas_call(
        _rmsnorm_body,
        grid=(m // tm,),
        in_specs=[pl.BlockSpec((tm, d), lambda i: (i, 0)), pl.BlockSpec((1, d), lambda i: (0, 0))],
        out_specs=pl.BlockSpec((tm, d), lambda i: (i, 0)),
        out_shape=jax.ShapeDtypeStruct((m, d), out_dtype),
        compiler_params=pltpu.CompilerParams(dimension_semantics=("parallel",), vmem_limit_bytes=VMEM_LIMIT),
        name="rmsnorm",
    )(x, g.reshape(1, d).astype(F32))


def _mm_body(*refs, nk, epilogue):
    if epilogue == "residual":
        x_ref, w_ref, r_ref, o_ref, acc_ref = refs
    else:
        x_ref, w_ref, o_ref, acc_ref = refs
    k = pl.program_id(2)

    @pl.when(k == 0)
    def _():
        acc_ref[...] = jnp.zeros_like(acc_ref)

    acc_ref[...] += jnp.dot(x_ref[...], w_ref[...], preferred_element_type=F32)

    @pl.when(k == nk - 1)
    def _():
        a = acc_ref[...]
        if epilogue == "relu2":
            a = jnp.square(jnp.maximum(a, 0.0))
        elif epilogue == "residual":
            a = a + r_ref[...]
        o_ref[...] = a.astype(o_ref.dtype)


def _matmul(x, w, layer, *, epilogue="none", res=None, out_dtype=F32, name="matmul"):
    m, kdim = x.shape
    n = w.shape[2]
    tm = _pick_tile(m, 1024, 8)
    tn = _pick_tile(n, 2304, MXU_DIM) if n % MXU_DIM == 0 else _pick_tile(n, 2304, LANE)
    tk = _pick_tile(kdim, 1024, LANE)
    nk = kdim // tk
    in_specs = [pl.BlockSpec((tm, tk), lambda i, j, k: (i, k)),
                pl.BlockSpec((None, tk, tn), lambda i, j, k: (layer, k, j))]
    args = [x, w]
    if epilogue == "residual":
        in_specs.append(pl.BlockSpec((tm, tn), lambda i, j, k: (i, j)))
        args.append(res)
    return pl.pallas_call(
        functools.partial(_mm_body, nk=nk, epilogue=epilogue),
        grid=(m // tm, n // tn, nk),
        in_specs=in_specs,
        out_specs=pl.BlockSpec((tm, tn), lambda i, j, k: (i, j)),
        out_shape=jax.ShapeDtypeStruct((m, n), out_dtype),
        scratch_shapes=[pltpu.VMEM((tm, tn), F32)],
        compiler_params=pltpu.CompilerParams(
            dimension_semantics=("parallel", "parallel", "arbitrary"), vmem_limit_bytes=VMEM_LIMIT),
        name=name,
    )(*args)


def _merge_body(*refs):
    o_refs, (w_ref, g_ref, out_ref, acc_ref) = refs[:N_BRANCH], refs[N_BRANCH:]
    b = pl.program_id(2)

    @pl.when(b == 0)
    def _():
        acc_ref[...] = jnp.zeros_like(acc_ref)

    for br in range(N_BRANCH):
        @pl.when(b == br)
        def _():
            acc_ref[...] += jax.nn.sigmoid(g_ref[...]) * jnp.dot(o_refs[br][...], w_ref[...],
                                                                 preferred_element_type=F32)

    @pl.when(b == N_BRANCH - 1)
    def _():
        out_ref[...] = acc_ref[...].astype(out_ref.dtype)


def _merge_branches(branches, w_br, layer, h, lay):
    m, mix = branches[0].shape
    d = lay.d_model
    tm = _pick_tile(m, 1024, 8)
    tn = _pick_tile(math.gcd(d, lay.off[22]), 1024, LANE)
    g0, gstep = lay.off[22] // tn, d // tn
    return pl.pallas_call(
        _merge_body,
        grid=(m // tm, d // tn, N_BRANCH),
        in_specs=[pl.BlockSpec((tm, mix), lambda i, j, b: (i, 0)) for _ in range(N_BRANCH)] + [
            pl.BlockSpec((None, None, mix, tn), lambda i, j, b: (layer, b, 0, j)),
            pl.BlockSpec((tm, tn), lambda i, j, b: (i, g0 + b * gstep + j))],
        out_specs=pl.BlockSpec((tm, tn), lambda i, j, b: (i, j)),
        out_shape=jax.ShapeDtypeStruct((m, d), BF16),
        scratch_shapes=[pltpu.VMEM((tm, tn), F32)],
        compiler_params=pltpu.CompilerParams(
            dimension_semantics=("parallel", "parallel", "arbitrary"), vmem_limit_bytes=VMEM_LIMIT),
        name="merge_branches",
    )(*branches, w_br, h)


def _l2norm(x):
    return x * lax.rsqrt(jnp.sum(x * x, axis=-1, keepdims=True) + EPS)


def _rms(x, g):
    y = x * lax.rsqrt(jnp.mean(x * x, axis=-1, keepdims=True) + EPS)
    return y * g


def _chunk_len(t, c):
    return c if t % c == 0 else t


def _rope(x, pos):
    half = x.shape[-1] // 2
    freq = jnp.exp(-math.log(ROPE_BASE) * jnp.arange(half, dtype=F32) / half)
    ang = pos.astype(F32)[:, None] * freq
    cos, sin = jnp.cos(ang)[None, :, None, :], jnp.sin(ang)[None, :, None, :]
    x1, x2 = x[..., :half], x[..., half:]
    return jnp.concatenate([x1 * cos - x2 * sin, x1 * sin + x2 * cos], axis=-1)


def _gated_delta_rule(q, k, v, g, beta, s0):
    b, t, h, _ = q.shape
    dv = v.shape[-1]
    c = _chunk_len(t, GDN_CHUNK)
    n = t // c
    ch = lambda a: jnp.swapaxes(a.reshape(b, n, c, h, *a.shape[3:]), 2, 3)
    q, k, v, g, beta = ch(q), ch(k), ch(v), ch(g), ch(beta)
    gc = jnp.cumsum(g, axis=-1)
    incl = jnp.tril(jnp.ones((c, c), bool))
    strict = jnp.tril(jnp.ones((c, c), F32), -1)
    decay = jnp.exp(jnp.where(incl, gc[..., :, None] - gc[..., None, :], -jnp.inf))
    kb = k * beta[..., None]
    a_mat = jnp.einsum('bnhid,bnhjd->bnhij', kb, k) * decay * strict
    eye = jnp.eye(c, dtype=F32)
    t_mat = lax.linalg.triangular_solve(a_mat + eye, jnp.broadcast_to(eye, a_mat.shape), left_side=True, lower=True)
    u = jnp.einsum('bnhij,bnhjd->bnhid', t_mat, v * beta[..., None])
    w = jnp.einsum('bnhij,bnhjd->bnhid', t_mat, kb * jnp.exp(gc)[..., None])
    qk = jnp.einsum('bnhid,bnhjd->bnhij', q, k) * decay
    qg = q * jnp.exp(gc)[..., None]
    kd = k * jnp.exp(gc[..., -1:] - gc)[..., None]
    glast = jnp.exp(gc[..., -1])

    def step(s, xs):
        u_i, w_i, qk_i, qg_i, kd_i, gl_i = xs
        v_new = u_i - jnp.einsum('bhcd,bhde->bhce', w_i, s)
        o = jnp.einsum('bhcd,bhde->bhce', qg_i, s) + jnp.einsum('bhij,bhje->bhie', qk_i, v_new)
        s = s * gl_i[..., None, None] + jnp.einsum('bhcd,bhce->bhde', kd_i, v_new)
        return s, o

    xs = tuple(jnp.moveaxis(a, 1, 0) for a in (u, w, qk, qg, kd, glast))
    s, o = lax.scan(step, s0, xs)
    return o.transpose(1, 0, 3, 2, 4).reshape(b, t, h, dv), s


def _gdn(qkv, z, b_raw, a_raw, conv_buf, s0, conv_w, a_log, dt_bias, norm_g, nh):
    b, t, _ = qkv.shape
    xc = jnp.concatenate([conv_buf, qkv], axis=1)
    y = jax.nn.silu(sum(xc[:, j:j + t] * conv_w[j] for j in range(GDN_CONV)))
    q, k, v = [a.reshape(b, t, nh, HEAD_DIM) for a in jnp.split(y, 3, axis=-1)]
    q = _l2norm(q) * HEAD_DIM ** -0.5
    k = _l2norm(k)
    beta = jax.nn.sigmoid(b_raw)
    g = -jnp.exp(a_log) * jax.nn.softplus(a_raw + dt_bias)
    o, s = _gated_delta_rule(q, k, v, g, beta, s0)
    o = _rms(o, norm_g) * jax.nn.silu(z.reshape(b, t, nh, HEAD_DIM))
    return o.reshape(b, t, nh * HEAD_DIM), xc[:, t:], s


def _retention(q, k, v, gate, pos, s0, gn_g, nh):
    b, t, _ = q.shape
    h, d = nh, HEAD_DIM
    q = _rope(q.reshape(b, t, h, d), pos)
    k = _rope(k.reshape(b, t, h, d), pos) * d ** -0.5
    v = v.reshape(b, t, h, d)
    lg = jnp.log1p(-jnp.exp2(-5.0 - jnp.arange(h, dtype=F32)))
    c = _chunk_len(t, RET_CHUNK)
    n = t // c
    qc, kc, vc = (a.reshape(b, n, c, h, d) for a in (q, k, v))
    i = jnp.arange(c, dtype=F32)
    diff = i[:, None] - i[None, :]
    dmat = jnp.where(diff >= 0, jnp.exp(jnp.maximum(diff, 0.0)[None] * lg[:, None, None]), 0.0)
    o_in = jnp.einsum('bnhij,bnjhe->bnihe', jnp.einsum('bnihd,bnjhd->bnhij', qc, kc) * dmat, vc)
    q_dec = qc * jnp.exp((i + 1.0)[:, None] * lg)[:, :, None]
    k_dec = kc * jnp.exp((c - 1.0 - i)[:, None] * lg)[:, :, None]
    kv = jnp.einsum('bnjhd,bnjhe->bnhde', k_dec, vc)
    chunk_decay = jnp.exp(c * lg)[:, None, None]

    def step(s, xs):
        qd_i, kv_i = xs
        o = jnp.einsum('bihd,bhde->bihe', qd_i, s)
        return s * chunk_decay + kv_i, o

    s, o_x = lax.scan(step, s0, (jnp.moveaxis(q_dec, 1, 0), jnp.moveaxis(kv, 1, 0)))
    o = (o_in + jnp.moveaxis(o_x, 0, 1)).reshape(b, t, h, d)
    oc = o - jnp.mean(o, axis=-1, keepdims=True)
    o = oc * lax.rsqrt(jnp.mean(oc * oc, axis=-1, keepdims=True) + EPS)
    return jax.nn.silu(gate) * (o.reshape(b, t, h * d) * gn_g), s


_NT = (((1,), (1,)), ((), ()))
GDN_PREP_ROWS = 256
RET_KERNEL_CHUNK = 128


def _iota(shape, dim):
    return lax.broadcasted_iota(jnp.int32, shape, dim)


def _col_block(off, width):
    assert off % width == 0, (off, width)
    return off // width


def _gdn_prep_body(x_ref, halo_ref, cb_ref, w_ref, y_ref, xc_ref, *, tb, nh):
    xc_ref[0:8, :] = jnp.where(pl.program_id(1) == 0, cb_ref[0], halo_ref[0])
    xc_ref[8:, :] = x_ref[0]
    for col in range(3 * nh):
        sl = slice(col * HEAD_DIM, (col + 1) * HEAD_DIM)
        acc = xc_ref[8:, sl] * w_ref[GDN_CONV - 1:GDN_CONV, sl]
        for j in range(GDN_CONV - 1):
            acc = acc + xc_ref[pl.ds(8 - (GDN_CONV - 1) + j, tb), sl] * w_ref[j:j + 1, sl]
        y = acc * jax.nn.sigmoid(acc)
        if col < 2 * nh:
            y = y * lax.rsqrt(jnp.sum(y * y, axis=-1, keepdims=True) + EPS)
            if col < nh:
                y = y * HEAD_DIM ** -0.5
        y_ref[0, :, sl] = y


def _gdn_prep(h3, conv_buf, conv_w, lay):
    b, t, _ = h3.shape
    w = 3 * lay.mix
    tb = _pick_tile(t, GDN_PREP_ROWS, 8)
    cb = jnp.pad(conv_buf, ((0, 0), (8 - conv_buf.shape[1], 0), (0, 0)))
    return pl.pallas_call(
        functools.partial(_gdn_prep_body, tb=tb, nh=lay.nh),
        grid=(b, t // tb),
        in_specs=[pl.BlockSpec((1, tb, w), lambda i, n: (i, n, _col_block(lay.off[0], w))),
                  pl.BlockSpec((1, 8, w), lambda i, n: (i, jnp.maximum(n * (tb // 8) - 1, 0), _col_block(lay.off[0], w))),
                  pl.BlockSpec((1, 8, w), lambda i, n: (i, 0, 0)),
                  pl.BlockSpec((GDN_CONV, w), lambda i, n: (0, 0))],
        out_specs=pl.BlockSpec((1, tb, w), lambda i, n: (i, n, 0)),
        out_shape=jax.ShapeDtypeStruct((b, t, w), F32),
        scratch_shapes=[pltpu.VMEM((tb + 8, w), F32)],
        compiler_params=pltpu.CompilerParams(dimension_semantics=("parallel", "parallel"),
                                             vmem_limit_bytes=VMEM_LIMIT),
        name="gdn_prep",
    )(h3, h3, cb, conv_w)


def _split3(x):
    hi = x.astype(BF16)
    rem = x - hi.astype(F32)
    mid = rem.astype(BF16)
    return hi, mid, (rem - mid.astype(F32)).astype(BF16)


def _mm_hi(x, y):
    xh, xl, _ = _split3(x)
    yh, yl, _ = _split3(y)
    dot = lambda a, b: jnp.dot(a, b, preferred_element_type=F32)
    return dot(xh, yh) + dot(xh, yl) + dot(xl, yh)


def _gdn_body(q_ref, k_ref, v_ref, z_ref, sm_ref, par_ref, ng_ref, s0_ref, o_ref, sout_ref, s_ref, *, c, nh, n_chunks):
    n = pl.program_id(1)

    @pl.when(n == 0)
    def _():
        s_ref[...] = s0_ref[0]

    sm = sm_ref[0]
    x = sm + par_ref[1:2, :]
    softplus = jnp.maximum(x, 0.0) + jnp.log(1.0 + jnp.exp(-jnp.abs(x)))
    gc = -jnp.exp(par_ref[0:1, :]) * softplus
    beta_all = jax.nn.sigmoid(sm)
    row = _iota((c, LANE), 0)
    step = 1
    while step < c:
        gc = gc + jnp.where(row >= step, pltpu.roll(gc, step, 0), 0.0)
        step *= 2
    gct = gc.T
    ii, jj = _iota((c, c), 0), _iota((c, c), 1)
    incl, strict = ii >= jj, ii > jj
    eye = jnp.where(ii == jj, 1.0, 0.0)
    dot = lambda a, b: jnp.dot(a, b, preferred_element_type=F32)
    heads = range(nh)
    sls = [slice(h * HEAD_DIM, (h + 1) * HEAD_DIM) for h in heads]
    gcol = [gc[:, nh + h:nh + h + 1] for h in heads]
    glast = [gc[c - 1:c, nh + h:nh + h + 1] for h in heads]
    beta = [beta_all[:, h:h + 1] for h in heads]
    a, qk, rhs, lhs_q, kd_t = [], [], [], [], []
    for h in heads:
        decay = jnp.where(incl, jnp.exp(jnp.where(incl, gcol[h] - gct[nh + h:nh + h + 1, :], 0.0)), 0.0)
        q, k, v = q_ref[0, :, sls[h]], k_ref[0, :, sls[h]], v_ref[0, :, sls[h]]
        kb = k * beta[h]
        eg = jnp.exp(gcol[h])
        kk = lax.dot_general(jnp.concatenate([kb, q], axis=0).astype(BF16), k.astype(BF16), _NT,
                             preferred_element_type=F32)
        a.append(jnp.where(strict, kk[:c] * decay, 0.0))
        qk.append((kk[c:] * decay).astype(BF16))
        rhs.append(jnp.concatenate([v * beta[h], kb * eg], axis=1).astype(BF16))
        lhs_q.append((q * eg).astype(BF16))
        kd_t.append((k * jnp.exp(glast[h] - gcol[h])).T.astype(BF16))
    t_inv, p = [eye - a[h] for h in heads], a
    power = 2
    while power < c:
        p = [_mm_hi(p[h], p[h]) for h in heads]
        t_inv = [t_inv[h] + _mm_hi(t_inv[h], p[h]) for h in heads]
        power *= 2
    uw = [dot(t_inv[h].astype(BF16), rhs[h]) for h in heads]
    ws = [dot(jnp.concatenate([uw[h][:, HEAD_DIM:].astype(BF16), lhs_q[h]], axis=0), s_ref[h].astype(BF16))
          for h in heads]
    v_new = [(uw[h][:, :HEAD_DIM] - ws[h][:c]).astype(BF16) for h in heads]
    for h in heads:
        s_ref[h] = s_ref[h] * jnp.exp(glast[h]) + dot(kd_t[h], v_new[h])
    for h in heads:
        o = ws[h][c:] + dot(qk[h], v_new[h])
        o = o * lax.rsqrt(jnp.mean(o * o, axis=-1, keepdims=True) + EPS) * ng_ref[...]
        z = z_ref[0, :, sls[h]]
        o_ref[0, :, sls[h]] = (o * (z * jax.nn.sigmoid(z))).astype(o_ref.dtype)

    @pl.when(n == n_chunks - 1)
    def _():
        sout_ref[0] = s_ref[...]


def _gdn_chunks(y, h3, s0, a_log, dt_bias, norm_g, lay):
    b, t, _ = y.shape
    nh, mix = lay.nh, lay.mix
    c = GDN_CHUNK
    n_chunks = t // c
    par = jnp.zeros((2, LANE), F32).at[0, nh:2 * nh].set(a_log).at[1, nh:2 * nh].set(dt_bias)
    assert lay.off[3] - lay.off[2] == nh
    tok = lambda off, width: pl.BlockSpec((1, c, width), lambda i, n: (i, n, _col_block(off, width)))
    state = pl.BlockSpec((1, nh, HEAD_DIM, HEAD_DIM), lambda i, n: (i, 0, 0, 0))
    return pl.pallas_call(
        functools.partial(_gdn_body, c=c, nh=nh, n_chunks=n_chunks),
        grid=(b, n_chunks),
        in_specs=[tok(0, mix), tok(mix, mix), tok(2 * mix, mix), tok(lay.off[1], mix), tok(lay.off[2], LANE),
                  pl.BlockSpec((2, LANE), lambda i, n: (0, 0)), pl.BlockSpec((1, HEAD_DIM), lambda i, n: (0, 0)),
                  state],
        out_specs=[pl.BlockSpec((1, c, mix), lambda i, n: (i, n, 0)), state],
        out_shape=[jax.ShapeDtypeStruct((b, t, mix), BF16), jax.ShapeDtypeStruct(s0.shape, F32)],
        scratch_shapes=[pltpu.VMEM((nh, HEAD_DIM, HEAD_DIM), F32)],
        compiler_params=pltpu.CompilerParams(dimension_semantics=("parallel", "arbitrary"),
                                             vmem_limit_bytes=VMEM_LIMIT),
        name="gdn_chunks",
    )(y, y, y, h3, h3, par, norm_g.reshape(1, HEAD_DIM), s0)


def _ret_body(q_ref, k_ref, v_ref, g_ref, cos_ref, sin_ref, gn_ref, s0_ref, o_ref, sout_ref, s_ref, *, c, nh, n_chunks):
    n = pl.program_id(1)

    @pl.when(n == 0)
    def _():
        s_ref[...] = s0_ref[0]

    cos, sin = cos_ref[...], sin_ref[...]
    ii, jj = _iota((c, c), 0), _iota((c, c), 1)
    lower = ii >= jj
    dist = jnp.where(lower, ii - jj, 0).astype(F32)
    icol = _iota((c, 1), 0).astype(F32)
    dot = lambda a, b: jnp.dot(a, b, preferred_element_type=F32)
    rope = lambda x: x * cos + pltpu.roll(x, HEAD_DIM // 2, 1) * sin
    for h in range(nh):
        lg = math.log1p(-2.0 ** (-5.0 - h))
        sl = slice(h * HEAD_DIM, (h + 1) * HEAD_DIM)
        q = rope(q_ref[0, :, sl])
        k = rope(k_ref[0, :, sl]) * HEAD_DIM ** -0.5
        v_bf = v_ref[0, :, sl].astype(BF16)
        qk = lax.dot_general(q.astype(BF16), k.astype(BF16), _NT, preferred_element_type=F32)
        qk = qk * jnp.where(lower, jnp.exp(dist * lg), 0.0)
        s_bf = s_ref[h].astype(BF16)
        o = dot(qk.astype(BF16), v_bf) + dot((q * jnp.exp((icol + 1.0) * lg)).astype(BF16), s_bf)
        k_dec = k * jnp.exp((c - 1.0 - icol) * lg)
        s_ref[h] = s_ref[h] * math.exp(c * lg) + dot(k_dec.T.astype(BF16), v_bf)
        oc = o - jnp.mean(o, axis=-1, keepdims=True)
        o = oc * lax.rsqrt(jnp.mean(oc * oc, axis=-1, keepdims=True) + EPS)
        gate = g_ref[0, :, sl]
        o_ref[0, :, sl] = (gate * jax.nn.sigmoid(gate) * (o * gn_ref[:, sl])).astype(o_ref.dtype)

    @pl.when(n == n_chunks - 1)
    def _():
        sout_ref[0] = s_ref[...]


def _ret_chunks(h3, pos, s0, gn_g, lay):
    b, t, _ = h3.shape
    nh, mix = lay.nh, lay.mix
    c = _pick_tile(t, RET_KERNEL_CHUNK, 8)
    n_chunks = t // c
    half = HEAD_DIM // 2
    freq = jnp.exp(-math.log(ROPE_BASE) * jnp.arange(half, dtype=F32) / half)
    ang = pos.astype(F32)[:, None] * freq
    cos, sin = jnp.cos(ang), jnp.sin(ang)
    cos2, sin2 = jnp.concatenate([cos, cos], axis=-1), jnp.concatenate([-sin, sin], axis=-1)
    tok = lambda off: pl.BlockSpec((1, c, mix), lambda i, n: (i, n, _col_block(off, mix)))
    table = pl.BlockSpec((c, HEAD_DIM), lambda i, n: (n, 0))
    state = pl.BlockSpec((1, nh, HEAD_DIM, HEAD_DIM), lambda i, n: (i, 0, 0, 0))
    return pl.pallas_call(
        functools.partial(_ret_body, c=c, nh=nh, n_chunks=n_chunks),
        grid=(b, n_chunks),
        in_specs=[tok(lay.off[4]), tok(lay.off[5]), tok(lay.off[6]), tok(lay.off[7]), table, table,
                  pl.BlockSpec((1, mix), lambda i, n: (0, 0)), state],
        out_specs=[pl.BlockSpec((1, c, mix), lambda i, n: (i, n, 0)), state],
        out_shape=[jax.ShapeDtypeStruct((b, t, mix), BF16), jax.ShapeDtypeStruct(s0.shape, F32)],
        scratch_shapes=[pltpu.VMEM((nh, HEAD_DIM, HEAD_DIM), F32)],
        compiler_params=pltpu.CompilerParams(dimension_semantics=("parallel", "arbitrary"),
                                             vmem_limit_bytes=VMEM_LIMIT),
        name="retention_chunks",
    )(h3, h3, h3, h3, cos2, sin2, gn_g.reshape(1, mix), s0)


INT_MIN = -2 ** 31
PAD_SCORE = -3e38


def _sortable(x):
    b = pltpu.bitcast(x, jnp.int32)
    return b ^ ((b >> 31) & 0x7FFFFFFF)


def _kth_largest(key_ref, nc, k, n_live):
    def count_ge(t):
        if n_live is None:
            acc = jnp.where(key_ref[0] >= t, 1.0, 0.0)
            for c in range(1, nc):
                acc = acc + jnp.where(key_ref[c] >= t, 1.0, 0.0)
        else:
            acc = lax.fori_loop(0, n_live, lambda c, acc: acc + jnp.where(key_ref[c] >= t, 1.0, 0.0),
                                jnp.zeros(key_ref.shape[1:], F32))
        return jnp.sum(acc, axis=-1, keepdims=True)

    tq = key_ref.shape[1]
    zero = jnp.zeros((tq, 1), jnp.int32)
    base = jnp.where(count_ge(zero) >= k, zero, jnp.full((tq, 1), INT_MIN, jnp.int32))

    def body(i, base):
        cand = base | (jnp.int32(1) << (30 - i))
        return jnp.where(count_ge(cand) >= k, cand, base)

    return lax.fori_loop(0, 31, body, base)


def _topk_blocks(key_ref, nc, k, n_live=None):
    _, tq, lc = key_ref.shape
    thr = _kth_largest(key_ref, nc, k, n_live)
    n_gt = jnp.where(key_ref[0] > thr, 1.0, 0.0)
    for c in range(1, nc):
        n_gt = n_gt + jnp.where(key_ref[c] > thr, 1.0, 0.0)
    need = k - jnp.sum(n_gt, axis=-1, keepdims=True)
    tri = jnp.where(_iota((LANE, LANE), 0) <= _iota((LANE, LANE), 1), 1.0, 0.0).astype(BF16)
    carry = jnp.zeros((tq, 1), F32)
    for c in range(nc):
        for j in range(lc // LANE):
            kk = key_ref[c, :, j * LANE:(j + 1) * LANE]
            eq = kk == thr
            eqf = jnp.where(eq, 1.0, 0.0)
            prefix = jnp.dot(eqf.astype(BF16), tri, preferred_element_type=F32) + carry
            carry = carry + jnp.sum(eqf, axis=-1, keepdims=True)
            yield c, j, (kk > thr) | (eq & (prefix <= need))


def _dsa_select_body(qi_ref, sm_ref, ki_ref, o_ref, qh32_ref, qh_ref, d_ref, key_ref, *, tq, lc, nc, pos0, topk, wi_off):
    q_first = pos0 + pl.program_id(1) * tq
    qpos = q_first + _iota((tq, 1), 0)
    n_live = jnp.minimum(nc, (q_first + tq - 1) // lc + 1)
    qi = qi_ref[0]
    for h in range(IDX_HEADS):
        qh32_ref[h * tq:(h + 1) * tq, :] = qi[:, h * IDX_DIM:(h + 1) * IDX_DIM]
    qh_ref[...] = qh32_ref[...].astype(BF16)
    sm = sm_ref[0]
    for c in range(nc):
        @pl.when(c < n_live)
        def _():
            kic = ki_ref[0, c * lc:(c + 1) * lc, :IDX_DIM].astype(BF16)
            d_ref[...] = lax.dot_general(qh_ref[...], kic, _NT, preferred_element_type=F32)
            score = jnp.zeros((tq, lc), F32)
            for h in range(IDX_HEADS):
                score = score + sm[:, wi_off + h:wi_off + h + 1] * jnp.maximum(d_ref[h * tq:(h + 1) * tq, :], 0.0)
            kpos = c * lc + _iota((tq, lc), 1)
            key_ref[c] = _sortable(jnp.where(kpos <= qpos, score, NEG))

        @pl.when(c >= n_live)
        def _():
            key_ref[c] = _sortable(jnp.full((tq, lc), NEG, F32))
    pieces = []
    for c, j, sel in _topk_blocks(key_ref, nc, topk, n_live):
        kpos = c * lc + j * LANE + _iota((tq, LANE), 1)
        pieces.append(jnp.where(sel & (kpos <= qpos), 1.0, 0.0))
        if len(pieces) == lc // LANE:
            o_ref[0, c] = jnp.concatenate(pieces, axis=1).astype(BF16)
            pieces = []


def _dsa_select(hq, qi_off, sm_off, rows, ki_off, *, tq, lc, pos0, l_real, wi_off):
    b, t, _ = hq.shape
    l = rows.shape[1]
    nc = l // lc
    topk = min(DSA_TOPK_MAX, l_real // 4)
    qw = IDX_HEADS * IDX_DIM
    return pl.pallas_call(
        functools.partial(_dsa_select_body, tq=tq, lc=lc, nc=nc, pos0=pos0, topk=topk, wi_off=wi_off),
        grid=(b, t // tq),
        in_specs=[pl.BlockSpec((1, tq, qw), lambda i, q: (i, q, _col_block(qi_off, qw))),
                  pl.BlockSpec((1, tq, LANE), lambda i, q: (i, q, _col_block(sm_off, LANE))),
                  pl.BlockSpec((1, l, LANE), lambda i, q: (i, 0, _col_block(ki_off, LANE)))],
        out_specs=pl.BlockSpec((1, nc, tq, lc), lambda i, q: (i, 0, q, 0)),
        out_shape=jax.ShapeDtypeStruct((b, nc, t, lc), BF16),
        scratch_shapes=[pltpu.VMEM((IDX_HEADS * tq, IDX_DIM), F32), pltpu.VMEM((IDX_HEADS * tq, IDX_DIM), BF16),
                        pltpu.VMEM((IDX_HEADS * tq, lc), F32), pltpu.VMEM((nc, tq, lc), jnp.int32)],
        compiler_params=pltpu.CompilerParams(dimension_semantics=("parallel", "arbitrary"),
                                             vmem_limit_bytes=VMEM_LIMIT),
        name="dsa_select",
    )(hq, hq, rows)


def _nsa_select_body(q_ref, kc0_ref, kc1_ref, vc0_ref, vc1_ref, wk_ref, wv_ref, bk_ref, bv_ref, ov_ref, e_ref,
                     ocmp_ref, mask_ref, ck_ref, cv_ref, cacc_ref, key_ref, *, tq, lc, nc, pos0, l16, ns, n_sel, grp):
    qb = pl.program_id(1)

    @pl.when(qb == 0)
    def _():
        for srcs, w_ref, b_ref, dst in (((kc0_ref, kc1_ref), wk_ref, bk_ref, ck_ref),
                                        ((vc0_ref, vc1_ref), wv_ref, bv_ref, cv_ref)):
            for g in range(KV_HEADS):
                for c in range(CMP_STRIDE):
                    x = srcs[g][0, pl.ds(c, l16, stride=CMP_STRIDE), :].astype(BF16)
                    part = jnp.dot(x, w_ref[c], preferred_element_type=F32)
                    if c == 0:
                        cacc_ref[...] = part
                    else:
                        cacc_ref[...] += part
                acc = cacc_ref[...]
                summ = acc[:, :HEAD_DIM] + pltpu.roll(acc[:, HEAD_DIM:], l16 - 1, 0) + b_ref[...]
                dst[g] = summ.astype(BF16)

    qpos = pos0 + qb * tq + _iota((tq, 1), 0)
    vis = (_iota((tq, l16), 1) * CMP_STRIDE + (CMP_BLOCK - 1)) <= qpos
    jcol = _iota((tq, ns), 1)
    cur = qpos >> int(math.log2(SEL_BLOCK))
    forced = (jcol == 0) | (jcol == cur) | (jcol == cur - 1)
    admissible = jcol * SEL_BLOCK <= qpos
    scale = HEAD_DIM ** -0.5
    for g in range(KV_HEADS):
        imp_c = jnp.zeros((tq, l16), F32)
        for r in range(grp):
            hd = g * grp + r
            q = q_ref[0][:, hd * HEAD_DIM:(hd + 1) * HEAD_DIM].astype(BF16)
            s = lax.dot_general(q, ck_ref[g], _NT, preferred_element_type=F32) * scale
            s = jnp.where(vis, s, NEG)
            e = jnp.exp(s - jnp.max(s, axis=-1, keepdims=True))
            p = jnp.where(vis, e / jnp.sum(e, axis=-1, keepdims=True), 0.0)
            imp_c = imp_c + p
            ocmp_ref[0, :, hd * HEAD_DIM:(hd + 1) * HEAD_DIM] = jnp.dot(
                p.astype(BF16), cv_ref[g], preferred_element_type=F32)
        hi = imp_c.astype(BF16)
        rem = imp_c - hi.astype(F32)
        mid = rem.astype(BF16)
        lo = (rem - mid.astype(F32)).astype(BF16)
        ov = ov_ref[...]
        imp = (jnp.dot(hi, ov, preferred_element_type=F32) + jnp.dot(mid, ov, preferred_element_type=F32)
               + jnp.dot(lo, ov, preferred_element_type=F32))
        imp = jnp.where(admissible, jnp.where(forced, imp + FORCE_BONUS, imp), NEG)
        imp = jnp.where(jcol < n_sel, imp, PAD_SCORE)
        key_ref[0] = _sortable(imp)
        sel = jnp.concatenate([jnp.where(s_, 1.0, 0.0) for _, _, s_ in _topk_blocks(key_ref, 1, min(SEL_TOPN, n_sel))],
                              axis=1).astype(BF16)
        for c in range(nc):
            tok = jnp.dot(sel, e_ref[:, c * lc:(c + 1) * lc], preferred_element_type=F32)
            kpos = c * lc + _iota((tq, lc), 1)
            mask_ref[g, c] = jnp.where((tok > 0.5) & (kpos <= qpos), 1.0, 0.0).astype(BF16)


def _nsa_select(hq, q_off, rows, kc_off, vc_off, cwk, cwv, pek, pev, *, tq, lc, pos0, l_real, nh):
    b, t, _ = hq.shape
    l = rows.shape[1]
    nc = l // lc
    grp = nh // KV_HEADS
    n_cmp = (l_real - CMP_BLOCK) // CMP_STRIDE + 1
    l16 = n_cmp + 1
    n_sel = -(-l_real // SEL_BLOCK)
    ns = _round_up(n_sel, LANE)
    pack = lambda w: jnp.concatenate([w[:CMP_STRIDE], w[CMP_STRIDE:]], axis=-1).astype(BF16)
    bias = lambda pe, w: jnp.einsum('cd,cde->e', pe, w, precision=lax.Precision.HIGHEST).reshape(1, HEAD_DIM)
    c0 = np.arange(l16)[:, None] * CMP_STRIDE
    s0 = np.arange(ns)[None, :] * SEL_BLOCK
    overlap = jnp.asarray((c0 < s0 + SEL_BLOCK) & (c0 + CMP_BLOCK > s0), BF16)
    expand = jnp.asarray(np.arange(l)[None, :] // SEL_BLOCK == np.arange(ns)[:, None], BF16)
    qw = nh * HEAD_DIM
    lrows = CMP_STRIDE * l16
    const = lambda shape: pl.BlockSpec(shape, lambda i, q: (0,) * len(shape))
    head_rows = lambda off, g: pl.BlockSpec((1, lrows, HEAD_DIM), lambda i, q: (i, 0, _col_block(off, HEAD_DIM) + g))
    assert KV_HEADS == 2
    return pl.pallas_call(
        functools.partial(_nsa_select_body, tq=tq, lc=lc, nc=nc, pos0=pos0, l16=l16, ns=ns, n_sel=n_sel, grp=grp),
        grid=(b, t // tq),
        in_specs=[pl.BlockSpec((1, tq, qw), lambda i, q: (i, q, _col_block(q_off, qw))),
                  head_rows(kc_off, 0), head_rows(kc_off, 1), head_rows(vc_off, 0), head_rows(vc_off, 1),
                  const((CMP_STRIDE, HEAD_DIM, 2 * HEAD_DIM)), const((CMP_STRIDE, HEAD_DIM, 2 * HEAD_DIM)),
                  const((1, HEAD_DIM)), const((1, HEAD_DIM)), const((l16, ns)), const((ns, l))],
        out_specs=[pl.BlockSpec((1, tq, qw), lambda i, q: (i, q, 0)),
                   pl.BlockSpec((KV_HEADS, nc, tq, lc), lambda i, q: (i, 0, q, 0))],
        out_shape=[jax.ShapeDtypeStruct((b, t, qw), F32),
                   jax.ShapeDtypeStruct((b * KV_HEADS, nc, t, lc), BF16)],
        scratch_shapes=[pltpu.VMEM((KV_HEADS, l16, HEAD_DIM), BF16), pltpu.VMEM((KV_HEADS, l16, HEAD_DIM), BF16),
                        pltpu.VMEM((l16, 2 * HEAD_DIM), F32), pltpu.VMEM((1, tq, ns), jnp.int32)],
        compiler_params=pltpu.CompilerParams(dimension_semantics=("parallel", "arbitrary"),
                                             vmem_limit_bytes=VMEM_LIMIT),
        name="nsa_select",
    )(hq, rows, rows, rows, rows, pack(cwk), pack(cwv), bias(pek, cwk), bias(pev, cwv), overlap, expand)


def _attend_body(*refs, banded, tq, lc, nc, pos0, kpos0, grp):
    if banded:
        q_ref, k_ref, v_ref, o_ref, s_ref, acc_ref = refs
    else:
        q_ref, k_ref, v_ref, m_ref, o_ref, s_ref, acc_ref = refs
    q_first = pos0 + pl.program_id(2) * tq
    qpos = q_first + _iota((tq, 1), 0)
    c_hi = jnp.minimum(nc, (q_first + tq - 1 - kpos0) // lc + 1)
    c_lo = jnp.maximum(q_first - WINDOW - kpos0, 0) // lc if banded else 0
    scale = HEAD_DIM ** -0.5
    qs = [q_ref[0][:, r * HEAD_DIM:(r + 1) * HEAD_DIM].astype(BF16) for r in range(grp)]

    def scores(c, ms):
        kc = k_ref[0, pl.ds(pl.multiple_of(c * lc, lc), lc), :].astype(BF16)
        if banded:
            kpos = kpos0 + c * lc + _iota((tq, lc), 1)
            valid = (kpos <= qpos) & (qpos - kpos <= WINDOW)
        else:
            valid = m_ref[0, c].astype(F32) > 0.5
        out = []
        for r in range(grp):
            s = jnp.where(valid, lax.dot_general(qs[r], kc, _NT, preferred_element_type=F32) * scale, NEG)
            s_ref[r, c] = s
            out.append(jnp.maximum(ms[r], jnp.max(s, axis=-1, keepdims=True)))
        return tuple(out)

    ms = lax.fori_loop(c_lo, c_hi, scores, tuple(jnp.full((tq, 1), NEG, F32) for _ in range(grp)))
    acc_ref[...] = jnp.zeros_like(acc_ref)

    def values(c, ls):
        vc = v_ref[0, pl.ds(pl.multiple_of(c * lc, lc), lc), :].astype(BF16)
        out = []
        for r in range(grp):
            p = jnp.exp(s_ref[r, c] - ms[r])
            acc_ref[r] += jnp.dot(p.astype(BF16), vc, preferred_element_type=F32)
            out.append(ls[r] + jnp.sum(p, axis=-1, keepdims=True))
        return tuple(out)

    ls = lax.fori_loop(c_lo, c_hi, values, tuple(jnp.zeros((tq, 1), F32) for _ in range(grp)))
    for r in range(grp):
        o_ref[0, :, r * HEAD_DIM:(r + 1) * HEAD_DIM] = (acc_ref[r] / ls[r]).astype(o_ref.dtype)


def _attend(hq, q_off, kv, k_off, v_off, mask, *, tq, lc, pos0, kpos0, nh, mask_per_group=False, out_dtype=F32,
            name="attend"):
    b, t, _ = hq.shape
    l = kv.shape[1]
    nc = l // lc
    grp = nh // KV_HEADS
    qw = grp * HEAD_DIM
    banded = mask is None
    in_specs = [pl.BlockSpec((1, tq, qw), lambda i, g, q: (i, q, _col_block(q_off, qw) + g)),
                pl.BlockSpec((1, l, HEAD_DIM), lambda i, g, q: (i, 0, _col_block(k_off, HEAD_DIM) + g)),
                pl.BlockSpec((1, l, HEAD_DIM), lambda i, g, q: (i, 0, _col_block(v_off, HEAD_DIM) + g))]
    args = [hq, kv, kv]
    if not banded:
        if mask_per_group:
            in_specs.append(pl.BlockSpec((1, nc, tq, lc), lambda i, g, q: (i * KV_HEADS + g, 0, q, 0)))
        else:
            in_specs.append(pl.BlockSpec((1, nc, tq, lc), lambda i, g, q: (i, 0, q, 0)))
        args.append(mask)
    return pl.pallas_call(
        functools.partial(_attend_body, banded=banded, tq=tq, lc=lc, nc=nc, pos0=pos0, kpos0=kpos0, grp=grp),
        grid=(b, KV_HEADS, t // tq),
        in_specs=in_specs,
        out_specs=pl.BlockSpec((1, tq, qw), lambda i, g, q: (i, q, g)),
        out_shape=jax.ShapeDtypeStruct((b, t, nh * HEAD_DIM), out_dtype),
        scratch_shapes=[pltpu.VMEM((grp, nc, tq, lc), F32), pltpu.VMEM((grp, tq, HEAD_DIM), F32)],
        compiler_params=pltpu.CompilerParams(dimension_semantics=("parallel", "parallel", "arbitrary"),
                                             vmem_limit_bytes=VMEM_LIMIT),
        name=name,
    )(*args)


def _nsa_combine_body(a_ref, b_ref, c_ref, sm_ref, o_ref, *, gate_off, nh):
    gate = jax.nn.sigmoid(sm_ref[...])
    for hd in range(nh):
        sl = slice(hd * HEAD_DIM, (hd + 1) * HEAD_DIM)
        col = lambda br: gate[:, gate_off + br * nh + hd:gate_off + br * nh + hd + 1]
        o = col(0) * a_ref[:, sl] + col(1) * b_ref[:, sl] + col(2) * c_ref[:, sl]
        o_ref[:, sl] = o.astype(o_ref.dtype)


def _nsa_combine(o_cmp, o_sel, o_win, h, sm_off, gate_off, nh):
    m, w = o_cmp.shape
    tm = _pick_tile(m, 512, 8)
    blk = pl.BlockSpec((tm, w), lambda i: (i, 0))
    return pl.pallas_call(
        functools.partial(_nsa_combine_body, gate_off=gate_off, nh=nh),
        grid=(m // tm,),
        in_specs=[blk, blk, blk, pl.BlockSpec((tm, LANE), lambda i: (i, _col_block(sm_off, LANE)))],
        out_specs=blk,
        out_shape=jax.ShapeDtypeStruct((m, w), BF16),
        compiler_params=pltpu.CompilerParams(dimension_semantics=("parallel",), vmem_limit_bytes=VMEM_LIMIT),
        name="nsa_combine",
    )(o_cmp, o_sel, o_win, h)


def _sparse_mixers(h, hq, kv, kv_off, win, win_off, win_pos0, lay, cw, *, tq, pos0, l_real, t_real):
    cwk, cwv, pek, pev = cw
    nh = lay.nh
    b = hq.shape[0]
    lc = _pick_tile(kv.shape[1], 32 * 1024 // tq, LANE)
    lcw = _pick_tile(win.shape[1], 16 * 1024 // tq, LANE)
    sm_off = lay.off[2]
    offs = np.concatenate([[0], np.cumsum(PAGED_SIZES)]).tolist()
    common = dict(tq=tq, pos0=pos0, nh=nh)
    o_cmp, sel_mask = _nsa_select(hq, lay.off[8], kv, kv_off + offs[0], kv_off + offs[1], cwk, cwv, pek, pev,
                                  lc=lc, l_real=l_real, **common)
    o_sel = _attend(hq, lay.off[8], kv, kv_off + offs[2], kv_off + offs[3], sel_mask, lc=lc, kpos0=0,
                    mask_per_group=True, name="nsa_selected", **common)
    o_win = _attend(hq, lay.off[8], win, win_off, win_off + KVW, None, lc=lcw, kpos0=win_pos0,
                    name="nsa_window", **common)
    dsa_mask = _dsa_select(hq, lay.off[19], sm_off, kv, kv_off + offs[6], tq=tq, lc=lc, pos0=pos0, l_real=l_real,
                           wi_off=lay.off[21] - sm_off)
    o_d = _attend(hq, lay.off[16], kv, kv_off + offs[4], kv_off + offs[5], dsa_mask, lc=lc, kpos0=0,
                  out_dtype=BF16, name="dsa_attend", **common)
    unpad = lambda o: o[:, :t_real].reshape(b * t_real, nh * HEAD_DIM)
    o_c = _nsa_combine(unpad(o_cmp), unpad(o_sel), unpad(o_win), h, sm_off, lay.off[15] - sm_off, nh)
    return o_c, unpad(o_d)


def _gather_body(pt_ref, x_ref, *o_refs, n_pages):
    del pt_ref
    live = pl.program_id(1) < n_pages

    @pl.when(live)
    def _():
        for l, o_ref in enumerate(o_refs):
            for c0 in range(0, PAGED_W, LANE):
                cw = min(LANE, PAGED_W - c0)
                o_ref[0, :, c0:c0 + cw] = x_ref[0, l, c0:c0 + cw, :].T
            o_ref[0, :, PAGED_W:] = jnp.zeros((o_ref.shape[1], o_ref.shape[2] - PAGED_W), F32)

    @pl.when(jnp.logical_not(live))
    def _():
        for o_ref in o_refs:
            o_ref[...] = jnp.zeros(o_ref.shape, F32)


def _paged_gather(cache_kv, page_table, l_pad):
    _, page, depth, width = cache_kv.shape
    b, n_pages = page_table.shape
    wp = _round_up(width, LANE)
    cache_t = jnp.transpose(cache_kv, (0, 2, 3, 1))
    out = pl.BlockSpec((1, page, wp), lambda i, p, pt: (i, p, 0))
    return pl.pallas_call(
        functools.partial(_gather_body, n_pages=n_pages),
        grid_spec=pltpu.PrefetchScalarGridSpec(
            num_scalar_prefetch=1,
            grid=(b, l_pad // page),
            in_specs=[pl.BlockSpec((1, depth, width, page),
                                   lambda i, p, pt: (pt[i * n_pages + jnp.minimum(p, n_pages - 1)], 0, 0, 0))],
            out_specs=[out] * depth),
        out_shape=[jax.ShapeDtypeStruct((b, l_pad, wp), F32)] * depth,
        compiler_params=pltpu.CompilerParams(dimension_semantics=("parallel", "arbitrary"),
                                             vmem_limit_bytes=VMEM_LIMIT),
        name="paged_gather",
    )(page_table.reshape(-1), cache_t)


def _layer(x, q_pos, lw, layer, past, lay):
    (g_mix, g_mlp, w_in_p, conv_w, a_log, dt_bias, gdn_g, ret_g, cwk, cwv, pek, pev, w_br, w_o, w_up_l, w_down_l) = lw
    conv_buf, s_gdn, s_ret, win_buf, past_kv, past_len = past
    b, t, d = x.shape
    m = b * t
    mix, nh = lay.mix, lay.nh
    x2d = x.reshape(m, d)
    h = _matmul(_rmsnorm(x2d, g_mix, BF16), w_in_p, layer, name="in_proj")
    h3 = h.reshape(b, t, lay.width)
    seg = lambda idx: h3[:, :, lay.off[idx]:lay.off[idx] + lay.in_sizes[idx]]

    if t % GDN_CHUNK == 0:
        o_a, s_gdn_new = _gdn_chunks(_gdn_prep(h3, conv_buf, conv_w, lay), h3, s_gdn, a_log, dt_bias, gdn_g, lay)
        conv_new = jnp.concatenate([conv_buf, seg(0)[:, t - (GDN_CONV - 1):]], axis=1)[:, -(GDN_CONV - 1):]
        o_b, s_ret_new = _ret_chunks(h3, q_pos, s_ret, ret_g, lay)
    else:
        o_a, conv_new, s_gdn_new = _gdn(seg(0), seg(1), seg(2), seg(3), conv_buf, s_gdn, conv_w, a_log, dt_bias,
                                        gdn_g, nh)
        o_b, s_ret_new = _retention(seg(4), seg(5), seg(6), seg(7), q_pos, s_ret, ret_g, nh)

    new_rows = h3[:, :, lay.off[9]:lay.off[9] + PAGED_W]
    win_rows = h3[:, :, lay.off[13]:lay.off[13] + 2 * KVW]
    cw = (cwk, cwv, pek, pev)
    if past_kv is None:
        win_new = win_rows[:, t - min(WINDOW, t):]
        o_c, o_d = _sparse_mixers(h, h3, h3, lay.off[9], h3, lay.off[13], 0, lay, cw,
                                  tq=Q_BLOCK, pos0=0, l_real=t, t_real=t)
    else:
        wb = win_buf.shape[1]
        tq = _round_up(t, 8)
        pad_rows = lambda a, n: jnp.pad(a, ((0, 0), (0, n - a.shape[1]), (0, 0)))
        lane_pad = past_kv.shape[2] - PAGED_W
        kv = lax.dynamic_update_slice(past_kv, jnp.pad(new_rows, ((0, 0), (0, 0), (0, lane_pad))), (0, past_len, 0))
        win_all = jnp.concatenate([win_buf, win_rows], axis=1)
        win_new = win_all[:, t:]
        o_c, o_d = _sparse_mixers(h, pad_rows(h3, tq), kv, 0, pad_rows(win_all, _round_up(wb + t, LANE)), 0,
                                  past_len - wb, lay, cw, tq=tq, pos0=past_len, l_real=past_len + t, t_real=t)

    branches = [o.reshape(m, mix).astype(BF16) for o in (o_a, o_b, o_c, o_d)]
    mixed = _merge_branches(branches, w_br, layer, h, lay)
    x2 = _matmul(mixed, w_o, layer, epilogue="residual", res=x2d, name="out_proj")
    up = _matmul(_rmsnorm(x2, g_mlp, BF16), w_up_l, layer, epilogue="relu2", out_dtype=BF16, name="ffn_up")
    x3 = _matmul(up, w_down_l, layer, epilogue="residual", res=x2, name="ffn_down")
    return x3.reshape(b, t, d), (new_rows, conv_new, s_gdn_new, s_ret_new, win_new)


def kernel(x_prompt, x_sample, cache_kv, cache_nsa_window, state_gdn, state_gdn_conv, state_retention, page_table, norm_mix, norm_mlp, norm_final, w_in, gdn_conv_w, gdn_a_log, gdn_dt_bias, gdn_norm, ret_norm, nsa_cmp_wk, nsa_cmp_wv, nsa_cmp_pe_k, nsa_cmp_pe_v, w_branch, w_out, w_up, w_down):
    bp, tp, d = x_prompt.shape
    bs, ts, _ = x_sample.shape
    depth = w_in.shape[0]
    lay = _Layout(d)
    past_len = page_table.shape[1] * cache_kv.shape[1]
    pos_p = jnp.arange(tp, dtype=jnp.int32)
    pos_s = past_len + jnp.arange(ts, dtype=jnp.int32)
    zero_conv = jnp.zeros((bp, GDN_CONV - 1, 3 * lay.mix), F32)
    zero_state = jnp.zeros((bp, lay.nh, HEAD_DIM, HEAD_DIM), F32)
    w_in_p = _pack_w_in(w_in, lay)
    w_br_b, w_o_b, w_up_b, w_down_b = (a.astype(BF16) for a in (w_branch, w_out, w_up, w_down))
    past_kv = _paged_gather(cache_kv, page_table, _round_up(past_len + ts, 2 * LANE))
    hp, hs = x_prompt, x_sample
    out_p, out_s = [], []
    for l in range(depth):
        lw = (norm_mix[l], norm_mlp[l], w_in_p, gdn_conv_w[l], gdn_a_log[l], gdn_dt_bias[l], gdn_norm[l],
              ret_norm[l], nsa_cmp_wk[l], nsa_cmp_wv[l], nsa_cmp_pe_k[l], nsa_cmp_pe_v[l], w_br_b,
              w_o_b, w_up_b, w_down_b)
        hp, st_p = _layer(hp, pos_p, lw, l, (zero_conv, zero_state, zero_state, None, None, 0), lay)
        out_p.append(st_p)
        past_s = (state_gdn_conv[:, l], state_gdn[:, l], state_retention[:, l], cache_nsa_window[:, l],
                  past_kv[l], past_len)
        hs, st_s = _layer(hs, pos_s, lw, l, past_s, lay)
        out_s.append(st_s)
    y_prompt = _rmsnorm(hp.reshape(bp * tp, d), norm_final, F32).reshape(bp, tp, d)
    y_sample = _rmsnorm(hs.reshape(bs * ts, d), norm_final, F32).reshape(bs, ts, d)
    stack = lambda outs, j, axis: jnp.stack([o[j] for o in outs], axis=axis)
    return (y_prompt, y_sample,
            stack(out_p, 0, 2), stack(out_s, 0, 2),
            stack(out_p, 2, 1), stack(out_s, 2, 1),
            stack(out_p, 1, 1), stack(out_s, 1, 1),
            stack(out_p, 3, 1), stack(out_s, 3, 1),
            stack(out_p, 4, 1), stack(out_s, 4, 1))
```

```python
import functools
import math

import jax
import jax.numpy as jnp
import numpy as np
from jax import lax
from jax.experimental import pallas as pl
from jax.experimental.pallas import tpu as pltpu

HEAD_DIM = 128
N_BRANCH = 4
EPS = 1e-6
NEG = -1e30
GDN_CONV = 4
GDN_CHUNK = 64
RET_CHUNK = 64
ROPE_BASE = 10000.0
KV_HEADS = 2
CMP_BLOCK = 32
CMP_STRIDE = 16
SEL_BLOCK = 64
SEL_TOPN = 16
WINDOW = 512
FORCE_BONUS = 1e4
IDX_HEADS = 16
IDX_DIM = 64
DSA_TOPK_MAX = 256
Q_BLOCK = 128
KVW = KV_HEADS * HEAD_DIM
PAGED_SIZES = (KVW, KVW, KVW, KVW, KVW, KVW, IDX_DIM)
PAGED_W = sum(PAGED_SIZES)

LANE = 128
MXU_DIM = 256
VMEM_LIMIT = 56 * 1024 * 1024

F32 = jnp.float32
BF16 = jnp.bfloat16


def _round_up(n, m):
    return -(-n // m) * m


def _pick_tile(dim, target, align):
    best = None
    for t in range(align, min(dim, target) + 1, align):
        if dim % t == 0:
            best = t
    return best if best is not None else dim


class _Layout:
    def __init__(self, d_model):
        mix = d_model // N_BRANCH
        nh = mix // HEAD_DIM
        self.d_model, self.mix, self.nh = d_model, mix, nh
        self.in_sizes = (
            3 * mix, mix, nh, nh,
            mix, mix, mix, mix,
            mix, KVW, KVW, KVW, KVW, KVW, KVW, 3 * nh,
            mix, KVW, KVW, IDX_HEADS * IDX_DIM, IDX_DIM, IDX_HEADS,
            N_BRANCH * d_model)
        self.order = [0, 1, 4, 5, 6, 7, 8, 16, 19, 22, 9, 10, 11, 12, 17, 18, 20, None, 13, 14, 2, 3, 15, 21, None]
        self.off = {}
        pos = 0
        self.pads = []
        for idx in self.order:
            if idx is None:
                pad = _round_up(pos, LANE) - pos
                self.pads.append(pad)
                pos += pad
            else:
                self.off[idx] = pos
                pos += self.in_sizes[idx]
        self.width = pos


SUBLANE = 8


def _pack_body(x_ref, o_ref, *, tiles):
    for t, pieces in enumerate(tiles):
        rows = [jnp.zeros((n, LANE), F32) if s is None else x_ref[0, s:s + n, :] for s, n in pieces]
        tile = rows[0] if len(rows) == 1 else jnp.concatenate(rows, axis=0)
        o_ref[0, :, t * LANE:(t + 1) * LANE] = tile.T.astype(BF16)


def _pack_w_in(w_in, lay):
    depth, d, in_width = w_in.shape
    starts = np.concatenate([[0], np.cumsum(lay.in_sizes)]).tolist()
    tiles, pads = [[]], iter(lay.pads)
    for idx in lay.order:
        src, width = (None, next(pads)) if idx is None else (starts[idx], lay.in_sizes[idx])
        while width:
            room = LANE - sum(n for _, n in tiles[-1])
            if room == 0:
                tiles.append([])
                room = LANE
            n = min(room, width)
            assert n % SUBLANE == 0 and (src is None or src % SUBLANE == 0)
            if tiles[-1] and src is not None and tiles[-1][-1][0] is not None \
                    and tiles[-1][-1][0] + tiles[-1][-1][1] == src:
                tiles[-1][-1] = (tiles[-1][-1][0], tiles[-1][-1][1] + n)
            else:
                tiles[-1].append((src, n))
            src = None if src is None else src + n
            width -= n
    assert len(tiles) * LANE == lay.width
    w_t = jnp.swapaxes(w_in, 1, 2)
    return pl.pallas_call(
        functools.partial(_pack_body, tiles=tiles),
        grid=(depth, d // LANE),
        in_specs=[pl.BlockSpec((1, in_width, LANE), lambda l, i: (l, 0, i))],
        out_specs=pl.BlockSpec((1, LANE, lay.width), lambda l, i: (l, i, 0)),
        out_shape=jax.ShapeDtypeStruct((depth, d, lay.width), BF16),
        compiler_params=pltpu.CompilerParams(dimension_semantics=("parallel", "parallel"),
                                             vmem_limit_bytes=VMEM_LIMIT),
        name="pack_w_in",
    )(w_t)


def _rmsnorm_body(x_ref, g_ref, o_ref):
    x = x_ref[...]
    y = x * lax.rsqrt(jnp.mean(x * x, axis=-1, keepdims=True) + EPS)
    o_ref[...] = (y * g_ref[...]).astype(o_ref.dtype)


def _rmsnorm(x, g, out_dtype):
    m, d = x.shape
    tm = _pick_tile(m, 256, 8)
    return pl.pallas_call(
        _rmsnorm_body,
        grid=(m // tm,),
        in_specs=[pl.BlockSpec((tm, d), lambda i: (i, 0)), pl.BlockSpec((1, d), lambda i: (0, 0))],
        out_specs=pl.BlockSpec((tm, d), lambda i: (i, 0)),
        out_shape=jax.ShapeDtypeStruct((m, d), out_dtype),
        compiler_params=pltpu.CompilerParams(dimension_semantics=("parallel",), vmem_limit_bytes=VMEM_LIMIT),
        name="rmsnorm",
    )(x, g.reshape(1, d).astype(F32))


def _mm_body(*refs, nk, epilogue):
    if epilogue == "residual":
        x_ref, w_ref, r_ref, o_ref, acc_ref = refs
    else:
        x_ref, w_ref, o_ref, acc_ref = refs
    k = pl.program_id(2)
    dot = lambda: jnp.dot(x_ref[...], w_ref[...], preferred_element_type=F32)

    def finish(a):
        if epilogue == "relu2":
            a = jnp.square(jnp.maximum(a, 0.0))
        elif epilogue == "residual":
            a = a + r_ref[...]
        o_ref[...] = a.astype(o_ref.dtype)

    if nk == 1:
        finish(dot())
        return

    @pl.when(k == 0)
    def _():
        acc_ref[...] = dot()

    @pl.when((k > 0) & (k < nk - 1))
    def _():
        acc_ref[...] += dot()

    @pl.when(k == nk - 1)
    def _():
        finish(acc_ref[...] + dot())


def _matmul(x, w, layer, *, epilogue="none", res=None, out_dtype=F32, name="matmul"):
    m, kdim = x.shape
    n = w.shape[2]
    tm = _pick_tile(m, 1024, 8)
    tn = _pick_tile(n, 2304, MXU_DIM) if n % MXU_DIM == 0 else _pick_tile(n, 2304, LANE)
    tk = _pick_tile(kdim, 1024, LANE)
    nk = kdim // tk
    in_specs = [pl.BlockSpec((tm, tk), lambda i, j, k: (i, k)),
                pl.BlockSpec((None, tk, tn), lambda i, j, k: (layer, k, j))]
    args = [x, w]
    if epilogue == "residual":
        in_specs.append(pl.BlockSpec((tm, tn), lambda i, j, k: (i, j)))
        args.append(res)
    return pl.pallas_call(
        functools.partial(_mm_body, nk=nk, epilogue=epilogue),
        grid=(m // tm, n // tn, nk),
        in_specs=in_specs,
        out_specs=pl.BlockSpec((tm, tn), lambda i, j, k: (i, j)),
        out_shape=jax.ShapeDtypeStruct((m, n), out_dtype),
        scratch_shapes=[pltpu.VMEM((tm, tn), F32)],
        compiler_params=pltpu.CompilerParams(
            dimension_semantics=("parallel", "parallel", "arbitrary"), vmem_limit_bytes=VMEM_LIMIT),
        name=name,
    )(*args)


def _merge_body(*refs):
    o_refs, (w_ref, g_ref, out_ref, acc_ref) = refs[:N_BRANCH], refs[N_BRANCH:]
    b = pl.program_id(2)

    for br in range(N_BRANCH):
        @pl.when(b == br)
        def _():
            term = jax.nn.sigmoid(g_ref[...]) * jnp.dot(o_refs[br][...], w_ref[...], preferred_element_type=F32)
            if br == 0:
                acc_ref[...] = term
            elif br < N_BRANCH - 1:
                acc_ref[...] += term
            else:
                out_ref[...] = (acc_ref[...] + term).astype(out_ref.dtype)


def _merge_branches(branches, w_br, layer, h, lay):
    m, mix = branches[0].shape
    d = lay.d_model
    tm = _pick_tile(m, 1024, 8)
    tn = _pick_tile(math.gcd(d, lay.off[22]), 1024, LANE)
    g0, gstep = lay.off[22] // tn, d // tn
    return pl.pallas_call(
        _merge_body,
        grid=(m // tm, d // tn, N_BRANCH),
        in_specs=[pl.BlockSpec((tm, mix), lambda i, j, b: (i, 0)) for _ in range(N_BRANCH)] + [
            pl.BlockSpec((None, None, mix, tn), lambda i, j, b: (layer, b, 0, j)),
            pl.BlockSpec((tm, tn), lambda i, j, b: (i, g0 + b * gstep + j))],
        out_specs=pl.BlockSpec((tm, tn), lambda i, j, b: (i, j)),
        out_shape=jax.ShapeDtypeStruct((m, d), BF16),
        scratch_shapes=[pltpu.VMEM((tm, tn), F32)],
        compiler_params=pltpu.CompilerParams(
            dimension_semantics=("parallel", "parallel", "arbitrary"), vmem_limit_bytes=VMEM_LIMIT),
        name="merge_branches",
    )(*branches, w_br, h)


def _l2norm(x):
    return x * lax.rsqrt(jnp.sum(x * x, axis=-1, keepdims=True) + EPS)


def _rms(x, g):
    y = x * lax.rsqrt(jnp.mean(x * x, axis=-1, keepdims=True) + EPS)
    return y * g


def _chunk_len(t, c):
    return c if t % c == 0 else t


def _rope(x, pos):
    half = x.shape[-1] // 2
    freq = jnp.exp(-math.log(ROPE_BASE) * jnp.arange(half, dtype=F32) / half)
    ang = pos.astype(F32)[:, None] * freq
    cos, sin = jnp.cos(ang)[None, :, None, :], jnp.sin(ang)[None, :, None, :]
    x1, x2 = x[..., :half], x[..., half:]
    return jnp.concatenate([x1 * cos - x2 * sin, x1 * sin + x2 * cos], axis=-1)


def _gated_delta_rule(q, k, v, g, beta, s0):
    b, t, h, _ = q.shape
    dv = v.shape[-1]
    c = _chunk_len(t, GDN_CHUNK)
    n = t // c
    ch = lambda a: jnp.swapaxes(a.reshape(b, n, c, h, *a.shape[3:]), 2, 3)
    q, k, v, g, beta = ch(q), ch(k), ch(v), ch(g), ch(beta)
    gc = jnp.cumsum(g, axis=-1)
    incl = jnp.tril(jnp.ones((c, c), bool))
    strict = jnp.tril(jnp.ones((c, c), F32), -1)
    decay = jnp.exp(jnp.where(incl, gc[..., :, None] - gc[..., None, :], -jnp.inf))
    kb = k * beta[..., None]
    a_mat = jnp.einsum('bnhid,bnhjd->bnhij', kb, k) * decay * strict
    eye = jnp.eye(c, dtype=F32)
    t_mat = lax.linalg.triangular_solve(a_mat + eye, jnp.broadcast_to(eye, a_mat.shape), left_side=True, lower=True)
    u = jnp.einsum('bnhij,bnhjd->bnhid', t_mat, v * beta[..., None])
    w = jnp.einsum('bnhij,bnhjd->bnhid', t_mat, kb * jnp.exp(gc)[..., None])
    qk = jnp.einsum('bnhid,bnhjd->bnhij', q, k) * decay
    qg = q * jnp.exp(gc)[..., None]
    kd = k * jnp.exp(gc[..., -1:] - gc)[..., None]
    glast = jnp.exp(gc[..., -1])

    def step(s, xs):
        u_i, w_i, qk_i, qg_i, kd_i, gl_i = xs
        v_new = u_i - jnp.einsum('bhcd,bhde->bhce', w_i, s)
        o = jnp.einsum('bhcd,bhde->bhce', qg_i, s) + jnp.einsum('bhij,bhje->bhie', qk_i, v_new)
        s = s * gl_i[..., None, None] + jnp.einsum('bhcd,bhce->bhde', kd_i, v_new)
        return s, o

    xs = tuple(jnp.moveaxis(a, 1, 0) for a in (u, w, qk, qg, kd, glast))
    s, o = lax.scan(step, s0, xs)
    return o.transpose(1, 0, 3, 2, 4).reshape(b, t, h, dv), s


def _gdn(qkv, z, b_raw, a_raw, conv_buf, s0, conv_w, a_log, dt_bias, norm_g, nh):
    b, t, _ = qkv.shape
    xc = jnp.concatenate([conv_buf, qkv], axis=1)
    y = jax.nn.silu(sum(xc[:, j:j + t] * conv_w[j] for j in range(GDN_CONV)))
    q, k, v = [a.reshape(b, t, nh, HEAD_DIM) for a in jnp.split(y, 3, axis=-1)]
    q = _l2norm(q) * HEAD_DIM ** -0.5
    k = _l2norm(k)
    beta = jax.nn.sigmoid(b_raw)
    g = -jnp.exp(a_log) * jax.nn.softplus(a_raw + dt_bias)
    o, s = _gated_delta_rule(q, k, v, g, beta, s0)
    o = _rms(o, norm_g) * jax.nn.silu(z.reshape(b, t, nh, HEAD_DIM))
    return o.reshape(b, t, nh * HEAD_DIM), xc[:, t:], s


def _retention(q, k, v, gate, pos, s0, gn_g, nh):
    b, t, _ = q.shape
    h, d = nh, HEAD_DIM
    q = _rope(q.reshape(b, t, h, d), pos)
    k = _rope(k.reshape(b, t, h, d), pos) * d ** -0.5
    v = v.reshape(b, t, h, d)
    lg = jnp.log1p(-jnp.exp2(-5.0 - jnp.arange(h, dtype=F32)))
    c = _chunk_len(t, RET_CHUNK)
    n = t // c
    qc, kc, vc = (a.reshape(b, n, c, h, d) for a in (q, k, v))
    i = jnp.arange(c, dtype=F32)
    diff = i[:, None] - i[None, :]
    dmat = jnp.where(diff >= 0, jnp.exp(jnp.maximum(diff, 0.0)[None] * lg[:, None, None]), 0.0)
    o_in = jnp.einsum('bnhij,bnjhe->bnihe', jnp.einsum('bnihd,bnjhd->bnhij', qc, kc) * dmat, vc)
    q_dec = qc * jnp.exp((i + 1.0)[:, None] * lg)[:, :, None]
    k_dec = kc * jnp.exp((c - 1.0 - i)[:, None] * lg)[:, :, None]
    kv = jnp.einsum('bnjhd,bnjhe->bnhde', k_dec, vc)
    chunk_decay = jnp.exp(c * lg)[:, None, None]

    def step(s, xs):
        qd_i, kv_i = xs
        o = jnp.einsum('bihd,bhde->bihe', qd_i, s)
        return s * chunk_decay + kv_i, o

    s, o_x = lax.scan(step, s0, (jnp.moveaxis(q_dec, 1, 0), jnp.moveaxis(kv, 1, 0)))
    o = (o_in + jnp.moveaxis(o_x, 0, 1)).reshape(b, t, h, d)
    oc = o - jnp.mean(o, axis=-1, keepdims=True)
    o = oc * lax.rsqrt(jnp.mean(oc * oc, axis=-1, keepdims=True) + EPS)
    return jax.nn.silu(gate) * (o.reshape(b, t, h * d) * gn_g), s


_NT = (((1,), (1,)), ((), ()))
GDN_PREP_ROWS = 256
RET_KERNEL_CHUNK = 128


def _iota(shape, dim):
    return lax.broadcasted_iota(jnp.int32, shape, dim)


def _col_block(off, width):
    assert off % width == 0, (off, width)
    return off // width


def _gdn_prep_body(x_ref, halo_ref, cb_ref, w_ref, y_ref, xc_ref, *, tb, nh):
    xc_ref[0:8, :] = jnp.where(pl.program_id(1) == 0, cb_ref[0], halo_ref[0])
    xc_ref[8:, :] = x_ref[0]
    for col in range(3 * nh):
        sl = slice(col * HEAD_DIM, (col + 1) * HEAD_DIM)
        acc = xc_ref[8:, sl] * w_ref[GDN_CONV - 1:GDN_CONV, sl]
        for j in range(GDN_CONV - 1):
            acc = acc + xc_ref[pl.ds(8 - (GDN_CONV - 1) + j, tb), sl] * w_ref[j:j + 1, sl]
        y = acc * jax.nn.sigmoid(acc)
        if col < 2 * nh:
            y = y * lax.rsqrt(jnp.sum(y * y, axis=-1, keepdims=True) + EPS)
            if col < nh:
                y = y * HEAD_DIM ** -0.5
        y_ref[0, :, sl] = y


def _gdn_prep(h3, conv_buf, conv_w, lay):
    b, t, _ = h3.shape
    w = 3 * lay.mix
    tb = _pick_tile(t, GDN_PREP_ROWS, 8)
    cb = jnp.pad(conv_buf, ((0, 0), (8 - conv_buf.shape[1], 0), (0, 0)))
    return pl.pallas_call(
        functools.partial(_gdn_prep_body, tb=tb, nh=lay.nh),
        grid=(b, t // tb),
        in_specs=[pl.BlockSpec((1, tb, w), lambda i, n: (i, n, _col_block(lay.off[0], w))),
                  pl.BlockSpec((1, 8, w), lambda i, n: (i, jnp.maximum(n * (tb // 8) - 1, 0), _col_block(lay.off[0], w))),
                  pl.BlockSpec((1, 8, w), lambda i, n: (i, 0, 0)),
                  pl.BlockSpec((GDN_CONV, w), lambda i, n: (0, 0))],
        out_specs=pl.BlockSpec((1, tb, w), lambda i, n: (i, n, 0)),
        out_shape=jax.ShapeDtypeStruct((b, t, w), F32),
        scratch_shapes=[pltpu.VMEM((tb + 8, w), F32)],
        compiler_params=pltpu.CompilerParams(dimension_semantics=("parallel", "parallel"),
                                             vmem_limit_bytes=VMEM_LIMIT),
        name="gdn_prep",
    )(h3, h3, cb, conv_w)


def _split3(x):
    hi = x.astype(BF16)
    rem = x - hi.astype(F32)
    mid = rem.astype(BF16)
    return hi, mid, (rem - mid.astype(F32)).astype(BF16)


def _mm_hi(x, y):
    xh, xl, _ = _split3(x)
    yh, yl, _ = _split3(y)
    dot = lambda a, b: jnp.dot(a, b, preferred_element_type=F32)
    return dot(xh, yh) + dot(xh, yl) + dot(xl, yh)


def _gdn_body(q_ref, k_ref, v_ref, z_ref, sm_ref, par_ref, ng_ref, s0_ref, o_ref, sout_ref, s_ref, *, c, nh, n_chunks):
    n = pl.program_id(1)

    @pl.when(n == 0)
    def _():
        s_ref[...] = s0_ref[0]

    sm = sm_ref[0]
    x = sm + par_ref[1:2, :]
    softplus = jnp.maximum(x, 0.0) + jnp.log(1.0 + jnp.exp(-jnp.abs(x)))
    gc = -jnp.exp(par_ref[0:1, :]) * softplus
    beta_all = jax.nn.sigmoid(sm)
    row = _iota((c, LANE), 0)
    step = 1
    while step < c:
        gc = gc + jnp.where(row >= step, pltpu.roll(gc, step, 0), 0.0)
        step *= 2
    gct = gc.T
    ii, jj = _iota((c, c), 0), _iota((c, c), 1)
    incl, strict = ii >= jj, ii > jj
    eye = jnp.where(ii == jj, 1.0, 0.0)
    dot = lambda a, b: jnp.dot(a, b, preferred_element_type=F32)
    heads = range(nh)
    sls = [slice(h * HEAD_DIM, (h + 1) * HEAD_DIM) for h in heads]
    gcol = [gc[:, nh + h:nh + h + 1] for h in heads]
    glast = [gc[c - 1:c, nh + h:nh + h + 1] for h in heads]
    beta = [beta_all[:, h:h + 1] for h in heads]
    a, qk, rhs, lhs_q, kd_t = [], [], [], [], []
    for h in heads:
        decay = jnp.where(incl, jnp.exp(jnp.where(incl, gcol[h] - gct[nh + h:nh + h + 1, :], 0.0)), 0.0)
        q, k, v = q_ref[0, :, sls[h]], k_ref[0, :, sls[h]], v_ref[0, :, sls[h]]
        kb = k * beta[h]
        eg = jnp.exp(gcol[h])
        kk = lax.dot_general(jnp.concatenate([kb, q], axis=0).astype(BF16), k.astype(BF16), _NT,
                             preferred_element_type=F32)
        a.append(jnp.where(strict, kk[:c] * decay, 0.0))
        qk.append((kk[c:] * decay).astype(BF16))
        rhs.append(jnp.concatenate([v * beta[h], kb * eg], axis=1).astype(BF16))
        lhs_q.append((q * eg).astype(BF16))
        kd_t.append((k * jnp.exp(glast[h] - gcol[h])).T.astype(BF16))
    t_inv, p = [eye - a[h] for h in heads], a
    power = 2
    while power < c:
        p = [_mm_hi(p[h], p[h]) for h in heads]
        t_inv = [t_inv[h] + _mm_hi(t_inv[h], p[h]) for h in heads]
        power *= 2
    uw = [dot(t_inv[h].astype(BF16), rhs[h]) for h in heads]
    ws = [dot(jnp.concatenate([uw[h][:, HEAD_DIM:].astype(BF16), lhs_q[h]], axis=0), s_ref[h].astype(BF16))
          for h in heads]
    v_new = [(uw[h][:, :HEAD_DIM] - ws[h][:c]).astype(BF16) for h in heads]
    for h in heads:
        s_ref[h] = s_ref[h] * jnp.exp(glast[h]) + dot(kd_t[h], v_new[h])
    for h in heads:
        o = ws[h][c:] + dot(qk[h], v_new[h])
        o = o * lax.rsqrt(jnp.mean(o * o, axis=-1, keepdims=True) + EPS) * ng_ref[...]
        z = z_ref[0, :, sls[h]]
        o_ref[0, :, sls[h]] = (o * (z * jax.nn.sigmoid(z))).astype(o_ref.dtype)

    @pl.when(n == n_chunks - 1)
    def _():
        sout_ref[0] = s_ref[...]


def _gdn_chunks(y, h3, s0, a_log, dt_bias, norm_g, lay):
    b, t, _ = y.shape
    nh, mix = lay.nh, lay.mix
    c = GDN_CHUNK
    n_chunks = t // c
    par = jnp.zeros((2, LANE), F32).at[0, nh:2 * nh].set(a_log).at[1, nh:2 * nh].set(dt_bias)
    assert lay.off[3] - lay.off[2] == nh
    tok = lambda off, width: pl.BlockSpec((1, c, width), lambda i, n: (i, n, _col_block(off, width)))
    state = pl.BlockSpec((1, nh, HEAD_DIM, HEAD_DIM), lambda i, n: (i, 0, 0, 0))
    return pl.pallas_call(
        functools.partial(_gdn_body, c=c, nh=nh, n_chunks=n_chunks),
        grid=(b, n_chunks),
        in_specs=[tok(0, mix), tok(mix, mix), tok(2 * mix, mix), tok(lay.off[1], mix), tok(lay.off[2], LANE),
                  pl.BlockSpec((2, LANE), lambda i, n: (0, 0)), pl.BlockSpec((1, HEAD_DIM), lambda i, n: (0, 0)),
                  state],
        out_specs=[pl.BlockSpec((1, c, mix), lambda i, n: (i, n, 0)), state],
        out_shape=[jax.ShapeDtypeStruct((b, t, mix), BF16), jax.ShapeDtypeStruct(s0.shape, F32)],
        scratch_shapes=[pltpu.VMEM((nh, HEAD_DIM, HEAD_DIM), F32)],
        compiler_params=pltpu.CompilerParams(dimension_semantics=("parallel", "arbitrary"),
                                             vmem_limit_bytes=VMEM_LIMIT),
        name="gdn_chunks",
    )(y, y, y, h3, h3, par, norm_g.reshape(1, HEAD_DIM), s0)


def _ret_body(q_ref, k_ref, v_ref, g_ref, cos_ref, sin_ref, gn_ref, s0_ref, o_ref, sout_ref, s_ref, *, c, nh, n_chunks):
    n = pl.program_id(1)

    @pl.when(n == 0)
    def _():
        s_ref[...] = s0_ref[0]

    cos, sin = cos_ref[...], sin_ref[...]
    ii, jj = _iota((c, c), 0), _iota((c, c), 1)
    lower = ii >= jj
    dist = jnp.where(lower, ii - jj, 0).astype(F32)
    icol = _iota((c, 1), 0).astype(F32)
    dot = lambda a, b: jnp.dot(a, b, preferred_element_type=F32)
    rope = lambda x: x * cos + pltpu.roll(x, HEAD_DIM // 2, 1) * sin
    for h in range(nh):
        lg = math.log1p(-2.0 ** (-5.0 - h))
        sl = slice(h * HEAD_DIM, (h + 1) * HEAD_DIM)
        q = rope(q_ref[0, :, sl])
        k = rope(k_ref[0, :, sl]) * HEAD_DIM ** -0.5
        v_bf = v_ref[0, :, sl].astype(BF16)
        qk = lax.dot_general(q.astype(BF16), k.astype(BF16), _NT, preferred_element_type=F32)
        qk = qk * jnp.where(lower, jnp.exp(dist * lg), 0.0)
        s_bf = s_ref[h].astype(BF16)
        o = dot(qk.astype(BF16), v_bf) + dot((q * jnp.exp((icol + 1.0) * lg)).astype(BF16), s_bf)
        k_dec = k * jnp.exp((c - 1.0 - icol) * lg)
        s_ref[h] = s_ref[h] * math.exp(c * lg) + dot(k_dec.T.astype(BF16), v_bf)
        oc = o - jnp.mean(o, axis=-1, keepdims=True)
        o = oc * lax.rsqrt(jnp.mean(oc * oc, axis=-1, keepdims=True) + EPS)
        gate = g_ref[0, :, sl]
        o_ref[0, :, sl] = (gate * jax.nn.sigmoid(gate) * (o * gn_ref[:, sl])).astype(o_ref.dtype)

    @pl.when(n == n_chunks - 1)
    def _():
        sout_ref[0] = s_ref[...]


def _ret_chunks(h3, pos, s0, gn_g, lay):
    b, t, _ = h3.shape
    nh, mix = lay.nh, lay.mix
    c = _pick_tile(t, RET_KERNEL_CHUNK, 8)
    n_chunks = t // c
    half = HEAD_DIM // 2
    freq = jnp.exp(-math.log(ROPE_BASE) * jnp.arange(half, dtype=F32) / half)
    ang = pos.astype(F32)[:, None] * freq
    cos, sin = jnp.cos(ang), jnp.sin(ang)
    cos2, sin2 = jnp.concatenate([cos, cos], axis=-1), jnp.concatenate([-sin, sin], axis=-1)
    tok = lambda off: pl.BlockSpec((1, c, mix), lambda i, n: (i, n, _col_block(off, mix)))
    table = pl.BlockSpec((c, HEAD_DIM), lambda i, n: (n, 0))
    state = pl.BlockSpec((1, nh, HEAD_DIM, HEAD_DIM), lambda i, n: (i, 0, 0, 0))
    return pl.pallas_call(
        functools.partial(_ret_body, c=c, nh=nh, n_chunks=n_chunks),
        grid=(b, n_chunks),
        in_specs=[tok(lay.off[4]), tok(lay.off[5]), tok(lay.off[6]), tok(lay.off[7]), table, table,
                  pl.BlockSpec((1, mix), lambda i, n: (0, 0)), state],
        out_specs=[pl.BlockSpec((1, c, mix), lambda i, n: (i, n, 0)), state],
        out_shape=[jax.ShapeDtypeStruct((b, t, mix), BF16), jax.ShapeDtypeStruct(s0.shape, F32)],
        scratch_shapes=[pltpu.VMEM((nh, HEAD_DIM, HEAD_DIM), F32)],
        compiler_params=pltpu.CompilerParams(dimension_semantics=("parallel", "arbitrary"),
                                             vmem_limit_bytes=VMEM_LIMIT),
        name="retention_chunks",
    )(h3, h3, h3, h3, cos2, sin2, gn_g.reshape(1, mix), s0)


INT_MIN = -2 ** 31
PAD_SCORE = -3e38


def _sortable(x):
    b = pltpu.bitcast(x, jnp.int32)
    return b ^ ((b >> 31) & 0x7FFFFFFF)


def _kth_largest(key_ref, nc, k, n_live, two_bits):
    def count_ge(t):
        if n_live is None:
            acc = jnp.where(key_ref[0] >= t, 1.0, 0.0)
            for c in range(1, nc):
                acc = acc + jnp.where(key_ref[c] >= t, 1.0, 0.0)
        else:
            acc = lax.fori_loop(0, n_live, lambda c, acc: acc + jnp.where(key_ref[c] >= t, 1.0, 0.0),
                                jnp.zeros(key_ref.shape[1:], F32))
        return jnp.sum(acc, axis=-1, keepdims=True)

    tq = key_ref.shape[1]
    zero = jnp.zeros((tq, 1), jnp.int32)
    base = jnp.where(count_ge(zero) >= k, zero, jnp.full((tq, 1), INT_MIN, jnp.int32))

    def one_bit(bit, base):
        cand = base | (jnp.int32(1) << bit)
        return jnp.where(count_ge(cand) >= k, cand, base)

    if not two_bits:
        return lax.fori_loop(0, 31, lambda i, base: one_bit(30 - i, base), base)

    def two_bit_step(i, base):
        lo = 29 - 2 * i
        c1, c2, c3 = (base | (jnp.int32(v) << lo) for v in (1, 2, 3))
        n1, n2, n3 = count_ge(c1), count_ge(c2), count_ge(c3)
        return jnp.where(n3 >= k, c3, jnp.where(n2 >= k, c2, jnp.where(n1 >= k, c1, base)))

    return one_bit(0, lax.fori_loop(0, 15, two_bit_step, base))


def _topk_blocks(key_ref, nc, k, n_live=None, two_bits=False):
    _, tq, lc = key_ref.shape
    thr = _kth_largest(key_ref, nc, k, n_live, two_bits)
    n_gt = jnp.where(key_ref[0] > thr, 1.0, 0.0)
    for c in range(1, nc):
        n_gt = n_gt + jnp.where(key_ref[c] > thr, 1.0, 0.0)
    need = k - jnp.sum(n_gt, axis=-1, keepdims=True)
    tri = jnp.where(_iota((LANE, LANE), 0) <= _iota((LANE, LANE), 1), 1.0, 0.0).astype(BF16)
    carry = jnp.zeros((tq, 1), F32)
    for c in range(nc):
        for j in range(lc // LANE):
            kk = key_ref[c, :, j * LANE:(j + 1) * LANE]
            eq = kk == thr
            eqf = jnp.where(eq, 1.0, 0.0)
            prefix = jnp.dot(eqf.astype(BF16), tri, preferred_element_type=F32) + carry
            carry = carry + jnp.sum(eqf, axis=-1, keepdims=True)
            yield c, j, (kk > thr) | (eq & (prefix <= need))


def _dsa_select_body(qi_ref, sm_ref, ki_ref, o_ref, qh32_ref, qh_ref, d_ref, key_ref, *, tq, lc, nc, pos0, topk, wi_off):
    q_first = pos0 + pl.program_id(1) * tq
    qpos = q_first + _iota((tq, 1), 0)
    n_live = jnp.minimum(nc, (q_first + tq - 1) // lc + 1)
    qi = qi_ref[0]
    for h in range(IDX_HEADS):
        qh32_ref[h * tq:(h + 1) * tq, :] = qi[:, h * IDX_DIM:(h + 1) * IDX_DIM]
    qh_ref[...] = qh32_ref[...].astype(BF16)
    sm = sm_ref[0]
    for c in range(nc):
        @pl.when(c < n_live)
        def _():
            kic = ki_ref[0, c * lc:(c + 1) * lc, :IDX_DIM].astype(BF16)
            d_ref[...] = lax.dot_general(qh_ref[...], kic, _NT, preferred_element_type=F32)
            score = jnp.zeros((tq, lc), F32)
            for h in range(IDX_HEADS):
                score = score + sm[:, wi_off + h:wi_off + h + 1] * jnp.maximum(d_ref[h * tq:(h + 1) * tq, :], 0.0)
            kpos = c * lc + _iota((tq, lc), 1)
            key_ref[c] = _sortable(jnp.where(kpos <= qpos, score, NEG))

        @pl.when(c >= n_live)
        def _():
            key_ref[c] = _sortable(jnp.full((tq, lc), NEG, F32))
    pieces = []
    for c, j, sel in _topk_blocks(key_ref, nc, topk, n_live):
        kpos = c * lc + j * LANE + _iota((tq, LANE), 1)
        pieces.append(jnp.where(sel & (kpos <= qpos), 1.0, 0.0))
        if len(pieces) == lc // LANE:
            o_ref[0, c] = jnp.concatenate(pieces, axis=1).astype(BF16)
            pieces = []


def _dsa_select(hq, qi_off, sm_off, rows, ki_off, *, tq, lc, pos0, l_real, wi_off):
    b, t, _ = hq.shape
    l = rows.shape[1]
    nc = l // lc
    topk = min(DSA_TOPK_MAX, l_real // 4)
    qw = IDX_HEADS * IDX_DIM
    return pl.pallas_call(
        functools.partial(_dsa_select_body, tq=tq, lc=lc, nc=nc, pos0=pos0, topk=topk, wi_off=wi_off),
        grid=(b, t // tq),
        in_specs=[pl.BlockSpec((1, tq, qw), lambda i, q: (i, q, _col_block(qi_off, qw))),
                  pl.BlockSpec((1, tq, LANE), lambda i, q: (i, q, _col_block(sm_off, LANE))),
                  pl.BlockSpec((1, l, LANE), lambda i, q: (i, 0, _col_block(ki_off, LANE)))],
        out_specs=pl.BlockSpec((1, nc, tq, lc), lambda i, q: (i, 0, q, 0)),
        out_shape=jax.ShapeDtypeStruct((b, nc, t, lc), BF16),
        scratch_shapes=[pltpu.VMEM((IDX_HEADS * tq, IDX_DIM), F32), pltpu.VMEM((IDX_HEADS * tq, IDX_DIM), BF16),
                        pltpu.VMEM((IDX_HEADS * tq, lc), F32), pltpu.VMEM((nc, tq, lc), jnp.int32)],
        compiler_params=pltpu.CompilerParams(dimension_semantics=("parallel", "arbitrary"),
                                             vmem_limit_bytes=VMEM_LIMIT),
        name="dsa_select",
    )(hq, hq, rows)


def _nsa_select_body(q_ref, kc0_ref, kc1_ref, vc0_ref, vc1_ref, wk_ref, wv_ref, bk_ref, bv_ref, ov_ref, e_ref,
                     ocmp_ref, mask_ref, ck_ref, cv_ref, cacc_ref, key_ref, *, tq, lc, nc, pos0, l16, ns, n_sel, grp):
    qb = pl.program_id(1)

    @pl.when(qb == 0)
    def _():
        for srcs, w_ref, b_ref, dst in (((kc0_ref, kc1_ref), wk_ref, bk_ref, ck_ref),
                                        ((vc0_ref, vc1_ref), wv_ref, bv_ref, cv_ref)):
            for g in range(KV_HEADS):
                for c in range(CMP_STRIDE):
                    x = srcs[g][0, pl.ds(c, l16, stride=CMP_STRIDE), :].astype(BF16)
                    part = jnp.dot(x, w_ref[c], preferred_element_type=F32)
                    if c == 0:
                        cacc_ref[...] = part
                    else:
                        cacc_ref[...] += part
                acc = cacc_ref[...]
                summ = acc[:, :HEAD_DIM] + pltpu.roll(acc[:, HEAD_DIM:], l16 - 1, 0) + b_ref[...]
                dst[g] = summ.astype(BF16)

    qpos = pos0 + qb * tq + _iota((tq, 1), 0)
    vis = (_iota((tq, l16), 1) * CMP_STRIDE + (CMP_BLOCK - 1)) <= qpos
    jcol = _iota((tq, ns), 1)
    cur = qpos >> int(math.log2(SEL_BLOCK))
    forced = (jcol == 0) | (jcol == cur) | (jcol == cur - 1)
    admissible = jcol * SEL_BLOCK <= qpos
    scale = HEAD_DIM ** -0.5
    for g in range(KV_HEADS):
        imp_c = jnp.zeros((tq, l16), F32)
        for r in range(grp):
            hd = g * grp + r
            q = q_ref[0][:, hd * HEAD_DIM:(hd + 1) * HEAD_DIM].astype(BF16)
            s = lax.dot_general(q, ck_ref[g], _NT, preferred_element_type=F32) * scale
            s = jnp.where(vis, s, NEG)
            e = jnp.exp(s - jnp.max(s, axis=-1, keepdims=True))
            p = jnp.where(vis, e / jnp.sum(e, axis=-1, keepdims=True), 0.0)
            imp_c = imp_c + p
            ocmp_ref[0, :, hd * HEAD_DIM:(hd + 1) * HEAD_DIM] = jnp.dot(
                p.astype(BF16), cv_ref[g], preferred_element_type=F32)
        hi = imp_c.astype(BF16)
        rem = imp_c - hi.astype(F32)
        mid = rem.astype(BF16)
        lo = (rem - mid.astype(F32)).astype(BF16)
        ov = ov_ref[...]
        imp = (jnp.dot(hi, ov, preferred_element_type=F32) + jnp.dot(mid, ov, preferred_element_type=F32)
               + jnp.dot(lo, ov, preferred_element_type=F32))
        imp = jnp.where(admissible, jnp.where(forced, imp + FORCE_BONUS, imp), NEG)
        imp = jnp.where(jcol < n_sel, imp, PAD_SCORE)
        key_ref[0, g * tq:(g + 1) * tq, :] = _sortable(imp)
    sel_all = jnp.concatenate(
        [jnp.where(s_, 1.0, 0.0) for _, _, s_ in _topk_blocks(key_ref, 1, min(SEL_TOPN, n_sel), two_bits=True)],
        axis=1)
    for g in range(KV_HEADS):
        sel = sel_all[g * tq:(g + 1) * tq].astype(BF16)
        for c in range(nc):
            tok = jnp.dot(sel, e_ref[:, c * lc:(c + 1) * lc], preferred_element_type=F32)
            kpos = c * lc + _iota((tq, lc), 1)
            mask_ref[g, c] = jnp.where((tok > 0.5) & (kpos <= qpos), 1.0, 0.0).astype(BF16)


def _nsa_select(hq, q_off, rows, kc_off, vc_off, cwk, cwv, pek, pev, *, tq, lc, pos0, l_real, nh):
    b, t, _ = hq.shape
    l = rows.shape[1]
    nc = l // lc
    grp = nh // KV_HEADS
    n_cmp = (l_real - CMP_BLOCK) // CMP_STRIDE + 1
    l16 = n_cmp + 1
    n_sel = -(-l_real // SEL_BLOCK)
    ns = _round_up(n_sel, LANE)
    pack = lambda w: jnp.concatenate([w[:CMP_STRIDE], w[CMP_STRIDE:]], axis=-1).astype(BF16)
    bias = lambda pe, w: jnp.einsum('cd,cde->e', pe, w, precision=lax.Precision.HIGHEST).reshape(1, HEAD_DIM)
    c0 = np.arange(l16)[:, None] * CMP_STRIDE
    s0 = np.arange(ns)[None, :] * SEL_BLOCK
    overlap = jnp.asarray((c0 < s0 + SEL_BLOCK) & (c0 + CMP_BLOCK > s0), BF16)
    expand = jnp.asarray(np.arange(l)[None, :] // SEL_BLOCK == np.arange(ns)[:, None], BF16)
    qw = nh * HEAD_DIM
    lrows = CMP_STRIDE * l16
    const = lambda shape: pl.BlockSpec(shape, lambda i, q: (0,) * len(shape))
    head_rows = lambda off, g: pl.BlockSpec((1, lrows, HEAD_DIM), lambda i, q: (i, 0, _col_block(off, HEAD_DIM) + g))
    assert KV_HEADS == 2
    return pl.pallas_call(
        functools.partial(_nsa_select_body, tq=tq, lc=lc, nc=nc, pos0=pos0, l16=l16, ns=ns, n_sel=n_sel, grp=grp),
        grid=(b, t // tq),
        in_specs=[pl.BlockSpec((1, tq, qw), lambda i, q: (i, q, _col_block(q_off, qw))),
                  head_rows(kc_off, 0), head_rows(kc_off, 1), head_rows(vc_off, 0), head_rows(vc_off, 1),
                  const((CMP_STRIDE, HEAD_DIM, 2 * HEAD_DIM)), const((CMP_STRIDE, HEAD_DIM, 2 * HEAD_DIM)),
                  const((1, HEAD_DIM)), const((1, HEAD_DIM)), const((l16, ns)), const((ns, l))],
        out_specs=[pl.BlockSpec((1, tq, qw), lambda i, q: (i, q, 0)),
                   pl.BlockSpec((KV_HEADS, nc, tq, lc), lambda i, q: (i, 0, q, 0))],
        out_shape=[jax.ShapeDtypeStruct((b, t, qw), F32),
                   jax.ShapeDtypeStruct((b * KV_HEADS, nc, t, lc), BF16)],
        scratch_shapes=[pltpu.VMEM((KV_HEADS, l16, HEAD_DIM), BF16), pltpu.VMEM((KV_HEADS, l16, HEAD_DIM), BF16),
                        pltpu.VMEM((l16, 2 * HEAD_DIM), F32), pltpu.VMEM((1, KV_HEADS * tq, ns), jnp.int32)],
        compiler_params=pltpu.CompilerParams(dimension_semantics=("parallel", "arbitrary"),
                                             vmem_limit_bytes=VMEM_LIMIT),
        name="nsa_select",
    )(hq, rows, rows, rows, rows, pack(cwk), pack(cwv), bias(pek, cwk), bias(pev, cwv), overlap, expand)


def _attend_body(*refs, banded, tq, lc, nc, pos0, kpos0, grp):
    if banded:
        q_ref, k_ref, v_ref, o_ref, s_ref, acc_ref = refs
    else:
        q_ref, k_ref, v_ref, m_ref, o_ref, s_ref, acc_ref = refs
    q_first = pos0 + pl.program_id(2) * tq
    qpos = q_first + _iota((tq, 1), 0)
    c_hi = jnp.minimum(nc, (q_first + tq - 1 - kpos0) // lc + 1)
    c_lo = jnp.maximum(q_first - WINDOW - kpos0, 0) // lc if banded else 0
    scale = HEAD_DIM ** -0.5
    qs = [q_ref[0][:, r * HEAD_DIM:(r + 1) * HEAD_DIM].astype(BF16) for r in range(grp)]

    def scores(c, ms):
        kc = k_ref[0, pl.ds(pl.multiple_of(c * lc, lc), lc), :].astype(BF16)
        if banded:
            kpos = kpos0 + c * lc + _iota((tq, lc), 1)
            valid = (kpos <= qpos) & (qpos - kpos <= WINDOW)
        else:
            valid = m_ref[0, c].astype(F32) > 0.5
        out = []
        for r in range(grp):
            s = jnp.where(valid, lax.dot_general(qs[r], kc, _NT, preferred_element_type=F32) * scale, NEG)
            s_ref[r, c] = s
            out.append(jnp.maximum(ms[r], jnp.max(s, axis=-1, keepdims=True)))
        return tuple(out)

    ms = lax.fori_loop(c_lo, c_hi, scores, tuple(jnp.full((tq, 1), NEG, F32) for _ in range(grp)))
    acc_ref[...] = jnp.zeros_like(acc_ref)

    def values(c, ls):
        vc = v_ref[0, pl.ds(pl.multiple_of(c * lc, lc), lc), :].astype(BF16)
        out = []
        for r in range(grp):
            p = jnp.exp(s_ref[r, c] - ms[r])
            acc_ref[r] += jnp.dot(p.astype(BF16), vc, preferred_element_type=F32)
            out.append(ls[r] + jnp.sum(p, axis=-1, keepdims=True))
        return tuple(out)

    ls = lax.fori_loop(c_lo, c_hi, values, tuple(jnp.zeros((tq, 1), F32) for _ in range(grp)))
    for r in range(grp):
        o_ref[0, :, r * HEAD_DIM:(r + 1) * HEAD_DIM] = (acc_ref[r] / ls[r]).astype(o_ref.dtype)


def _attend(hq, q_off, kv, k_off, v_off, mask, *, tq, lc, pos0, kpos0, nh, mask_per_group=False, out_dtype=F32,
            name="attend"):
    b, t, _ = hq.shape
    l = kv.shape[1]
    nc = l // lc
    grp = nh // KV_HEADS
    qw = grp * HEAD_DIM
    banded = mask is None
    in_specs = [pl.BlockSpec((1, tq, qw), lambda i, g, q: (i, q, _col_block(q_off, qw) + g)),
                pl.BlockSpec((1, l, HEAD_DIM), lambda i, g, q: (i, 0, _col_block(k_off, HEAD_DIM) + g)),
                pl.BlockSpec((1, l, HEAD_DIM), lambda i, g, q: (i, 0, _col_block(v_off, HEAD_DIM) + g))]
    args = [hq, kv, kv]
    if not banded:
        if mask_per_group:
            in_specs.append(pl.BlockSpec((1, nc, tq, lc), lambda i, g, q: (i * KV_HEADS + g, 0, q, 0)))
        else:
            in_specs.append(pl.BlockSpec((1, nc, tq, lc), lambda i, g, q: (i, 0, q, 0)))
        args.append(mask)
    return pl.pallas_call(
        functools.partial(_attend_body, banded=banded, tq=tq, lc=lc, nc=nc, pos0=pos0, kpos0=kpos0, grp=grp),
        grid=(b, KV_HEADS, t // tq),
        in_specs=in_specs,
        out_specs=pl.BlockSpec((1, tq, qw), lambda i, g, q: (i, q, g)),
        out_shape=jax.ShapeDtypeStruct((b, t, nh * HEAD_DIM), out_dtype),
        scratch_shapes=[pltpu.VMEM((grp, nc, tq, lc), F32), pltpu.VMEM((grp, tq, HEAD_DIM), F32)],
        compiler_params=pltpu.CompilerParams(dimension_semantics=("parallel", "parallel", "arbitrary"),
                                             vmem_limit_bytes=VMEM_LIMIT),
        name=name,
    )(*args)


def _nsa_combine_body(a_ref, b_ref, c_ref, sm_ref, o_ref, *, gate_off, nh):
    gate = jax.nn.sigmoid(sm_ref[...])
    for hd in range(nh):
        sl = slice(hd * HEAD_DIM, (hd + 1) * HEAD_DIM)
        col = lambda br: gate[:, gate_off + br * nh + hd:gate_off + br * nh + hd + 1]
        o = col(0) * a_ref[:, sl] + col(1) * b_ref[:, sl] + col(2) * c_ref[:, sl]
        o_ref[:, sl] = o.astype(o_ref.dtype)


def _nsa_combine(o_cmp, o_sel, o_win, h, sm_off, gate_off, nh):
    m, w = o_cmp.shape
    tm = _pick_tile(m, 512, 8)
    blk = pl.BlockSpec((tm, w), lambda i: (i, 0))
    return pl.pallas_call(
        functools.partial(_nsa_combine_body, gate_off=gate_off, nh=nh),
        grid=(m // tm,),
        in_specs=[blk, blk, blk, pl.BlockSpec((tm, LANE), lambda i: (i, _col_block(sm_off, LANE)))],
        out_specs=blk,
        out_shape=jax.ShapeDtypeStruct((m, w), BF16),
        compiler_params=pltpu.CompilerParams(dimension_semantics=("parallel",), vmem_limit_bytes=VMEM_LIMIT),
        name="nsa_combine",
    )(o_cmp, o_sel, o_win, h)


def _sparse_mixers(h, hq, kv, kv_off, win, win_off, win_pos0, lay, cw, *, tq, pos0, l_real, t_real):
    cwk, cwv, pek, pev = cw
    nh = lay.nh
    b = hq.shape[0]
    lc = _pick_tile(kv.shape[1], 32 * 1024 // tq, LANE)
    lcw = _pick_tile(win.shape[1], 16 * 1024 // tq, LANE)
    sm_off = lay.off[2]
    offs = np.concatenate([[0], np.cumsum(PAGED_SIZES)]).tolist()
    common = dict(tq=tq, pos0=pos0, nh=nh)
    o_cmp, sel_mask = _nsa_select(hq, lay.off[8], kv, kv_off + offs[0], kv_off + offs[1], cwk, cwv, pek, pev,
                                  lc=lc, l_real=l_real, **common)
    o_sel = _attend(hq, lay.off[8], kv, kv_off + offs[2], kv_off + offs[3], sel_mask, lc=lc, kpos0=0,
                    mask_per_group=True, name="nsa_selected", **common)
    o_win = _attend(hq, lay.off[8], win, win_off, win_off + KVW, None, lc=lcw, kpos0=win_pos0,
                    name="nsa_window", **common)
    dsa_mask = _dsa_select(hq, lay.off[19], sm_off, kv, kv_off + offs[6], tq=tq, lc=lc, pos0=pos0, l_real=l_real,
                           wi_off=lay.off[21] - sm_off)
    o_d = _attend(hq, lay.off[16], kv, kv_off + offs[4], kv_off + offs[5], dsa_mask, lc=lc, kpos0=0,
                  out_dtype=BF16, name="dsa_attend", **common)
    unpad = lambda o: o[:, :t_real].reshape(b * t_real, nh * HEAD_DIM)
    o_c = _nsa_combine(unpad(o_cmp), unpad(o_sel), unpad(o_win), h, sm_off, lay.off[15] - sm_off, nh)
    return o_c, unpad(o_d)


def _gather_body(pt_ref, x_ref, *o_refs, n_pages):
    del pt_ref
    live = pl.program_id(1) < n_pages

    @pl.when(live)
    def _():
        for l, o_ref in enumerate(o_refs):
            for c0 in range(0, PAGED_W, LANE):
                cw = min(LANE, PAGED_W - c0)
                o_ref[0, :, c0:c0 + cw] = x_ref[0, l, c0:c0 + cw, :].T
            o_ref[0, :, PAGED_W:] = jnp.zeros((o_ref.shape[1], o_ref.shape[2] - PAGED_W), F32)

    @pl.when(jnp.logical_not(live))
    def _():
        for o_ref in o_refs:
            o_ref[...] = jnp.zeros(o_ref.shape, F32)


def _paged_gather(cache_kv, page_table, l_pad):
    _, page, depth, width = cache_kv.shape
    b, n_pages = page_table.shape
    wp = _round_up(width, LANE)
    cache_t = jnp.transpose(cache_kv, (0, 2, 3, 1))
    out = pl.BlockSpec((1, page, wp), lambda i, p, pt: (i, p, 0))
    return pl.pallas_call(
        functools.partial(_gather_body, n_pages=n_pages),
        grid_spec=pltpu.PrefetchScalarGridSpec(
            num_scalar_prefetch=1,
            grid=(b, l_pad // page),
            in_specs=[pl.BlockSpec((1, depth, width, page),
                                   lambda i, p, pt: (pt[i * n_pages + jnp.minimum(p, n_pages - 1)], 0, 0, 0))],
            out_specs=[out] * depth),
        out_shape=[jax.ShapeDtypeStruct((b, l_pad, wp), F32)] * depth,
        compiler_params=pltpu.CompilerParams(dimension_semantics=("parallel", "arbitrary"),
                                             vmem_limit_bytes=VMEM_LIMIT),
        name="paged_gather",
    )(page_table.reshape(-1), cache_t)


def _layer(x, q_pos, lw, layer, past, lay):
    (g_mix, g_mlp, w_in_p, conv_w, a_log, dt_bias, gdn_g, ret_g, cwk, cwv, pek, pev, w_br, w_o, w_up_l, w_down_l) = lw
    conv_buf, s_gdn, s_ret, win_buf, past_kv, past_len = past
    b, t, d = x.shape
    m = b * t
    mix, nh = lay.mix, lay.nh
    x2d = x.reshape(m, d)
    h = _matmul(_rmsnorm(x2d, g_mix, BF16), w_in_p, layer, name="in_proj")
    h3 = h.reshape(b, t, lay.width)
    seg = lambda idx: h3[:, :, lay.off[idx]:lay.off[idx] + lay.in_sizes[idx]]

    if t % GDN_CHUNK == 0:
        o_a, s_gdn_new = _gdn_chunks(_gdn_prep(h3, conv_buf, conv_w, lay), h3, s_gdn, a_log, dt_bias, gdn_g, lay)
        conv_new = jnp.concatenate([conv_buf, seg(0)[:, t - (GDN_CONV - 1):]], axis=1)[:, -(GDN_CONV - 1):]
        o_b, s_ret_new = _ret_chunks(h3, q_pos, s_ret, ret_g, lay)
    else:
        o_a, conv_new, s_gdn_new = _gdn(seg(0), seg(1), seg(2), seg(3), conv_buf, s_gdn, conv_w, a_log, dt_bias,
                                        gdn_g, nh)
        o_b, s_ret_new = _retention(seg(4), seg(5), seg(6), seg(7), q_pos, s_ret, ret_g, nh)

    new_rows = h3[:, :, lay.off[9]:lay.off[9] + PAGED_W]
    win_rows = h3[:, :, lay.off[13]:lay.off[13] + 2 * KVW]
    cw = (cwk, cwv, pek, pev)
    if past_kv is None:
        win_new = win_rows[:, t - min(WINDOW, t):]
        o_c, o_d = _sparse_mixers(h, h3, h3, lay.off[9], h3, lay.off[13], 0, lay, cw,
                                  tq=Q_BLOCK, pos0=0, l_real=t, t_real=t)
    else:
        wb = win_buf.shape[1]
        tq = _round_up(t, 8)
        pad_rows = lambda a, n: jnp.pad(a, ((0, 0), (0, n - a.shape[1]), (0, 0)))
        lane_pad = past_kv.shape[2] - PAGED_W
        kv = lax.dynamic_update_slice(past_kv, jnp.pad(new_rows, ((0, 0), (0, 0), (0, lane_pad))), (0, past_len, 0))
        win_all = jnp.concatenate([win_buf, win_rows], axis=1)
        win_new = win_all[:, t:]
        o_c, o_d = _sparse_mixers(h, pad_rows(h3, tq), kv, 0, pad_rows(win_all, _round_up(wb + t, LANE)), 0,
                                  past_len - wb, lay, cw, tq=tq, pos0=past_len, l_real=past_len + t, t_real=t)

    branches = [o.reshape(m, mix).astype(BF16) for o in (o_a, o_b, o_c, o_d)]
    mixed = _merge_branches(branches, w_br, layer, h, lay)
    x2 = _matmul(mixed, w_o, layer, epilogue="residual", res=x2d, name="out_proj")
    up = _matmul(_rmsnorm(x2, g_mlp, BF16), w_up_l, layer, epilogue="relu2", out_dtype=BF16, name="ffn_up")
    x3 = _matmul(up, w_down_l, layer, epilogue="residual", res=x2, name="ffn_down")
    return x3.reshape(b, t, d), (new_rows, conv_new, s_gdn_new, s_ret_new, win_new)


def kernel(x_prompt, x_sample, cache_kv, cache_nsa_window, state_gdn, state_gdn_conv, state_retention, page_table, norm_mix, norm_mlp, norm_final, w_in, gdn_conv_w, gdn_a_log, gdn_dt_bias, gdn_norm, ret_norm, nsa_cmp_wk, nsa_cmp_wv, nsa_cmp_pe_k, nsa_cmp_pe_v, w_branch, w_out, w_up, w_down):
    bp, tp, d = x_prompt.shape
    bs, ts, _ = x_sample.shape
    depth = w_in.shape[0]
    lay = _Layout(d)
    past_len = page_table.shape[1] * cache_kv.shape[1]
    pos_p = jnp.arange(tp, dtype=jnp.int32)
    pos_s = past_len + jnp.arange(ts, dtype=jnp.int32)
    zero_conv = jnp.zeros((bp, GDN_CONV - 1, 3 * lay.mix), F32)
    zero_state = jnp.zeros((bp, lay.nh, HEAD_DIM, HEAD_DIM), F32)
    w_in_p = _pack_w_in(w_in, lay)
    w_br_b, w_o_b, w_up_b, w_down_b = (a.astype(BF16) for a in (w_branch, w_out, w_up, w_down))
    past_kv = _paged_gather(cache_kv, page_table, _round_up(past_len + ts, 2 * LANE))
    hp, hs = x_prompt, x_sample
    out_p, out_s = [], []
    for l in range(depth):
        lw = (norm_mix[l], norm_mlp[l], w_in_p, gdn_conv_w[l], gdn_a_log[l], gdn_dt_bias[l], gdn_norm[l],
              ret_norm[l], nsa_cmp_wk[l], nsa_cmp_wv[l], nsa_cmp_pe_k[l], nsa_cmp_pe_v[l], w_br_b,
              w_o_b, w_up_b, w_down_b)
        hp, st_p = _layer(hp, pos_p, lw, l, (zero_conv, zero_state, zero_state, None, None, 0), lay)
        out_p.append(st_p)
        past_s = (state_gdn_conv[:, l], state_gdn[:, l], state_retention[:, l], cache_nsa_window[:, l],
                  past_kv[l], past_len)
        hs, st_s = _layer(hs, pos_s, lw, l, past_s, lay)
        out_s.append(st_s)
    y_prompt = _rmsnorm(hp.reshape(bp * tp, d), norm_final, F32).reshape(bp, tp, d)
    y_sample = _rmsnorm(hs.reshape(bs * ts, d), norm_final, F32).reshape(bs, ts, d)
    stack = lambda outs, j, axis: jnp.stack([o[j] for o in outs], axis=axis)
    return (y_prompt, y_sample,
            stack(out_p, 0, 2), stack(out_s, 0, 2),
            stack(out_p, 2, 1), stack(out_s, 2, 1),
            stack(out_p, 1, 1), stack(out_s, 1, 1),
            stack(out_p, 3, 1), stack(out_s, 3, 1),
            stack(out_p, 4, 1), stack(out_s, 4, 1))
```

```python
import functools
import math

import jax
import jax.numpy as jnp
import numpy as np
from jax import lax
from jax.experimental import pallas as pl
from jax.experimental.pallas import tpu as pltpu

HEAD_DIM = 128
N_BRANCH = 4
EPS = 1e-6
NEG = -1e30
GDN_CONV = 4
GDN_CHUNK = 64
RET_CHUNK = 64
ROPE_BASE = 10000.0
KV_HEADS = 2
CMP_BLOCK = 32
CMP_STRIDE = 16
SEL_BLOCK = 64
SEL_TOPN = 16
WINDOW = 512
FORCE_BONUS = 1e4
IDX_HEADS = 16
IDX_DIM = 64
DSA_TOPK_MAX = 256
Q_BLOCK = 128
KVW = KV_HEADS * HEAD_DIM
PAGED_SIZES = (KVW, KVW, KVW, KVW, KVW, KVW, IDX_DIM)
PAGED_W = sum(PAGED_SIZES)

LANE = 128
MXU_DIM = 256
VMEM_LIMIT = 56 * 1024 * 1024

F32 = jnp.float32
BF16 = jnp.bfloat16


def _round_up(n, m):
    return -(-n // m) * m


def _pick_tile(dim, target, align):
    best = None
    for t in range(align, min(dim, target) + 1, align):
        if dim % t == 0:
            best = t
    return best if best is not None else dim


class _Layout:
    def __init__(self, d_model):
        mix = d_model // N_BRANCH
        nh = mix // HEAD_DIM
        self.d_model, self.mix, self.nh = d_model, mix, nh
        self.in_sizes = (
            3 * mix, mix, nh, nh,
            mix, mix, mix, mix,
            mix, KVW, KVW, KVW, KVW, KVW, KVW, 3 * nh,
            mix, KVW, KVW, IDX_HEADS * IDX_DIM, IDX_DIM, IDX_HEADS,
            N_BRANCH * d_model)
        self.order = [0, 1, 4, 5, 6, 7, 8, 16, 19, 22, 9, 10, 11, 12, 17, 18, 20, None, 13, 14, 2, 3, 15, 21, None]
        self.off = {}
        pos = 0
        self.pads = []
        for idx in self.order:
            if idx is None:
                pad = _round_up(pos, LANE) - pos
                self.pads.append(pad)
                pos += pad
            else:
                self.off[idx] = pos
                pos += self.in_sizes[idx]
        self.width = pos


SUBLANE = 8


def _pack_body(x_ref, o_ref, *, tiles):
    for t, pieces in enumerate(tiles):
        rows = [jnp.zeros((n, LANE), F32) if s is None else x_ref[0, s:s + n, :] for s, n in pieces]
        tile = rows[0] if len(rows) == 1 else jnp.concatenate(rows, axis=0)
        o_ref[0, :, t * LANE:(t + 1) * LANE] = tile.T.astype(BF16)


def _pack_w_in(w_in, lay):
    depth, d, in_width = w_in.shape
    starts = np.concatenate([[0], np.cumsum(lay.in_sizes)]).tolist()
    tiles, pads = [[]], iter(lay.pads)
    for idx in lay.order:
        src, width = (None, next(pads)) if idx is None else (starts[idx], lay.in_sizes[idx])
        while width:
            room = LANE - sum(n for _, n in tiles[-1])
            if room == 0:
                tiles.append([])
                room = LANE
            n = min(room, width)
            assert n % SUBLANE == 0 and (src is None or src % SUBLANE == 0)
            if tiles[-1] and src is not None and tiles[-1][-1][0] is not None \
                    and tiles[-1][-1][0] + tiles[-1][-1][1] == src:
                tiles[-1][-1] = (tiles[-1][-1][0], tiles[-1][-1][1] + n)
            else:
                tiles[-1].append((src, n))
            src = None if src is None else src + n
            width -= n
    assert len(tiles) * LANE == lay.width
    w_t = jnp.swapaxes(w_in, 1, 2)
    return pl.pallas_call(
        functools.partial(_pack_body, tiles=tiles),
        grid=(depth, d // LANE),
        in_specs=[pl.BlockSpec((1, in_width, LANE), lambda l, i: (l, 0, i))],
        out_specs=pl.BlockSpec((1, LANE, lay.width), lambda l, i: (l, i, 0)),
        out_shape=jax.ShapeDtypeStruct((depth, d, lay.width), BF16),
        compiler_params=pltpu.CompilerParams(dimension_semantics=("parallel", "parallel"),
                                             vmem_limit_bytes=VMEM_LIMIT),
        name="pack_w_in",
    )(w_t)


def _rmsnorm_body(x_ref, g_ref, o_ref):
    x = x_ref[...]
    y = x * lax.rsqrt(jnp.mean(x * x, axis=-1, keepdims=True) + EPS)
    o_ref[...] = (y * g_ref[...]).astype(o_ref.dtype)


def _rmsnorm(x, g, out_dtype):
    m, d = x.shape
    tm = _pick_tile(m, 256, 8)
    return pl.pallas_call(
        _rmsnorm_body,
        grid=(m // tm,),
        in_specs=[pl.BlockSpec((tm, d), lambda i: (i, 0)), pl.BlockSpec((1, d), lambda i: (0, 0))],
        out_specs=pl.BlockSpec((tm, d), lambda i: (i, 0)),
        out_shape=jax.ShapeDtypeStruct((m, d), out_dtype),
        compiler_params=pltpu.CompilerParams(dimension_semantics=("parallel",), vmem_limit_bytes=VMEM_LIMIT),
        name="rmsnorm",
    )(x, g.reshape(1, d).astype(F32))


def _mm_body(*refs, nk, epilogue):
    if epilogue == "residual":
        x_ref, w_ref, r_ref, o_ref, acc_ref = refs
    else:
        x_ref, w_ref, o_ref, acc_ref = refs
    k = pl.program_id(2)
    dot = lambda: jnp.dot(x_ref[...], w_ref[...], preferred_element_type=F32)

    def finish(a):
        if epilogue == "relu2":
            a = jnp.square(jnp.maximum(a, 0.0))
        elif epilogue == "residual":
            a = a + r_ref[...]
        o_ref[...] = a.astype(o_ref.dtype)

    if nk == 1:
        finish(dot())
        return

    @pl.when(k == 0)
    def _():
        acc_ref[...] = dot()

    @pl.when((k > 0) & (k < nk - 1))
    def _():
        acc_ref[...] += dot()

    @pl.when(k == nk - 1)
    def _():
        finish(acc_ref[...] + dot())


def _matmul(x, w, layer, *, epilogue="none", res=None, out_dtype=F32, name="matmul"):
    m, kdim = x.shape
    n = w.shape[2]
    tm = _pick_tile(m, 1024, 8)
    tn = _pick_tile(n, 2304, MXU_DIM) if n % MXU_DIM == 0 else _pick_tile(n, 2304, LANE)
    tk = _pick_tile(kdim, 1024, LANE)
    nk = kdim // tk
    in_specs = [pl.BlockSpec((tm, tk), lambda i, j, k: (i, k)),
                pl.BlockSpec((None, tk, tn), lambda i, j, k: (layer, k, j))]
    args = [x, w]
    if epilogue == "residual":
        in_specs.append(pl.BlockSpec((tm, tn), lambda i, j, k: (i, j)))
        args.append(res)
    return pl.pallas_call(
        functools.partial(_mm_body, nk=nk, epilogue=epilogue),
        grid=(m // tm, n // tn, nk),
        in_specs=in_specs,
        out_specs=pl.BlockSpec((tm, tn), lambda i, j, k: (i, j)),
        out_shape=jax.ShapeDtypeStruct((m, n), out_dtype),
        scratch_shapes=[pltpu.VMEM((tm, tn), F32)],
        compiler_params=pltpu.CompilerParams(
            dimension_semantics=("parallel", "parallel", "arbitrary"), vmem_limit_bytes=VMEM_LIMIT),
        name=name,
    )(*args)


def _merge_body(*refs):
    o_refs, (w_ref, g_ref, out_ref, acc_ref) = refs[:N_BRANCH], refs[N_BRANCH:]
    b = pl.program_id(2)

    for br in range(N_BRANCH):
        @pl.when(b == br)
        def _():
            term = jax.nn.sigmoid(g_ref[...]) * jnp.dot(o_refs[br][...], w_ref[...], preferred_element_type=F32)
            if br == 0:
                acc_ref[...] = term
            elif br < N_BRANCH - 1:
                acc_ref[...] += term
            else:
                out_ref[...] = (acc_ref[...] + term).astype(out_ref.dtype)


def _merge_branches(branches, w_br, layer, h, lay):
    m, mix = branches[0].shape
    d = lay.d_model
    tm = _pick_tile(m, 1024, 8)
    tn = _pick_tile(math.gcd(d, lay.off[22]), 1024, LANE)
    g0, gstep = lay.off[22] // tn, d // tn
    return pl.pallas_call(
        _merge_body,
        grid=(m // tm, d // tn, N_BRANCH),
        in_specs=[pl.BlockSpec((tm, mix), lambda i, j, b: (i, 0)) for _ in range(N_BRANCH)] + [
            pl.BlockSpec((None, None, mix, tn), lambda i, j, b: (layer, b, 0, j)),
            pl.BlockSpec((tm, tn), lambda i, j, b: (i, g0 + b * gstep + j))],
        out_specs=pl.BlockSpec((tm, tn), lambda i, j, b: (i, j)),
        out_shape=jax.ShapeDtypeStruct((m, d), BF16),
        scratch_shapes=[pltpu.VMEM((tm, tn), F32)],
        compiler_params=pltpu.CompilerParams(
            dimension_semantics=("parallel", "parallel", "arbitrary"), vmem_limit_bytes=VMEM_LIMIT),
        name="merge_branches",
    )(*branches, w_br, h)


def _l2norm(x):
    return x * lax.rsqrt(jnp.sum(x * x, axis=-1, keepdims=True) + EPS)


def _rms(x, g):
    y = x * lax.rsqrt(jnp.mean(x * x, axis=-1, keepdims=True) + EPS)
    return y * g


def _chunk_len(t, c):
    return c if t % c == 0 else t


def _rope(x, pos):
    half = x.shape[-1] // 2
    freq = jnp.exp(-math.log(ROPE_BASE) * jnp.arange(half, dtype=F32) / half)
    ang = pos.astype(F32)[:, None] * freq
    cos, sin = jnp.cos(ang)[None, :, None, :], jnp.sin(ang)[None, :, None, :]
    x1, x2 = x[..., :half], x[..., half:]
    return jnp.concatenate([x1 * cos - x2 * sin, x1 * sin + x2 * cos], axis=-1)


def _gated_delta_rule(q, k, v, g, beta, s0):
    b, t, h, _ = q.shape
    dv = v.shape[-1]
    c = _chunk_len(t, GDN_CHUNK)
    n = t // c
    ch = lambda a: jnp.swapaxes(a.reshape(b, n, c, h, *a.shape[3:]), 2, 3)
    q, k, v, g, beta = ch(q), ch(k), ch(v), ch(g), ch(beta)
    gc = jnp.cumsum(g, axis=-1)
    incl = jnp.tril(jnp.ones((c, c), bool))
    strict = jnp.tril(jnp.ones((c, c), F32), -1)
    decay = jnp.exp(jnp.where(incl, gc[..., :, None] - gc[..., None, :], -jnp.inf))
    kb = k * beta[..., None]
    a_mat = jnp.einsum('bnhid,bnhjd->bnhij', kb, k) * decay * strict
    eye = jnp.eye(c, dtype=F32)
    t_mat = lax.linalg.triangular_solve(a_mat + eye, jnp.broadcast_to(eye, a_mat.shape), left_side=True, lower=True)
    u = jnp.einsum('bnhij,bnhjd->bnhid', t_mat, v * beta[..., None])
    w = jnp.einsum('bnhij,bnhjd->bnhid', t_mat, kb * jnp.exp(gc)[..., None])
    qk = jnp.einsum('bnhid,bnhjd->bnhij', q, k) * decay
    qg = q * jnp.exp(gc)[..., None]
    kd = k * jnp.exp(gc[..., -1:] - gc)[..., None]
    glast = jnp.exp(gc[..., -1])

    def step(s, xs):
        u_i, w_i, qk_i, qg_i, kd_i, gl_i = xs
        v_new = u_i - jnp.einsum('bhcd,bhde->bhce', w_i, s)
        o = jnp.einsum('bhcd,bhde->bhce', qg_i, s) + jnp.einsum('bhij,bhje->bhie', qk_i, v_new)
        s = s * gl_i[..., None, None] + jnp.einsum('bhcd,bhce->bhde', kd_i, v_new)
        return s, o

    xs = tuple(jnp.moveaxis(a, 1, 0) for a in (u, w, qk, qg, kd, glast))
    s, o = lax.scan(step, s0, xs)
    return o.transpose(1, 0, 3, 2, 4).reshape(b, t, h, dv), s


def _gdn(qkv, z, b_raw, a_raw, conv_buf, s0, conv_w, a_log, dt_bias, norm_g, nh):
    b, t, _ = qkv.shape
    xc = jnp.concatenate([conv_buf, qkv], axis=1)
    y = jax.nn.silu(sum(xc[:, j:j + t] * conv_w[j] for j in range(GDN_CONV)))
    q, k, v = [a.reshape(b, t, nh, HEAD_DIM) for a in jnp.split(y, 3, axis=-1)]
    q = _l2norm(q) * HEAD_DIM ** -0.5
    k = _l2norm(k)
    beta = jax.nn.sigmoid(b_raw)
    g = -jnp.exp(a_log) * jax.nn.softplus(a_raw + dt_bias)
    o, s = _gated_delta_rule(q, k, v, g, beta, s0)
    o = _rms(o, norm_g) * jax.nn.silu(z.reshape(b, t, nh, HEAD_DIM))
    return o.reshape(b, t, nh * HEAD_DIM), xc[:, t:], s


def _retention(q, k, v, gate, pos, s0, gn_g, nh):
    b, t, _ = q.shape
    h, d = nh, HEAD_DIM
    q = _rope(q.reshape(b, t, h, d), pos)
    k = _rope(k.reshape(b, t, h, d), pos) * d ** -0.5
    v = v.reshape(b, t, h, d)
    lg = jnp.log1p(-jnp.exp2(-5.0 - jnp.arange(h, dtype=F32)))
    c = _chunk_len(t, RET_CHUNK)
    n = t // c
    qc, kc, vc = (a.reshape(b, n, c, h, d) for a in (q, k, v))
    i = jnp.arange(c, dtype=F32)
    diff = i[:, None] - i[None, :]
    dmat = jnp.where(diff >= 0, jnp.exp(jnp.maximum(diff, 0.0)[None] * lg[:, None, None]), 0.0)
    o_in = jnp.einsum('bnhij,bnjhe->bnihe', jnp.einsum('bnihd,bnjhd->bnhij', qc, kc) * dmat, vc)
    q_dec = qc * jnp.exp((i + 1.0)[:, None] * lg)[:, :, None]
    k_dec = kc * jnp.exp((c - 1.0 - i)[:, None] * lg)[:, :, None]
    kv = jnp.einsum('bnjhd,bnjhe->bnhde', k_dec, vc)
    chunk_decay = jnp.exp(c * lg)[:, None, None]

    def step(s, xs):
        qd_i, kv_i = xs
        o = jnp.einsum('bihd,bhde->bihe', qd_i, s)
        return s * chunk_decay + kv_i, o

    s, o_x = lax.scan(step, s0, (jnp.moveaxis(q_dec, 1, 0), jnp.moveaxis(kv, 1, 0)))
    o = (o_in + jnp.moveaxis(o_x, 0, 1)).reshape(b, t, h, d)
    oc = o - jnp.mean(o, axis=-1, keepdims=True)
    o = oc * lax.rsqrt(jnp.mean(oc * oc, axis=-1, keepdims=True) + EPS)
    return jax.nn.silu(gate) * (o.reshape(b, t, h * d) * gn_g), s


_NT = (((1,), (1,)), ((), ()))
GDN_PREP_ROWS = 256
RET_KERNEL_CHUNK = 128


def _iota(shape, dim):
    return lax.broadcasted_iota(jnp.int32, shape, dim)


def _col_block(off, width):
    assert off % width == 0, (off, width)
    return off // width


def _gdn_prep_body(x_ref, halo_ref, cb_ref, w_ref, y_ref, xc_ref, *, tb, nh):
    xc_ref[0:8, :] = jnp.where(pl.program_id(1) == 0, cb_ref[0], halo_ref[0])
    xc_ref[8:, :] = x_ref[0]
    for col in range(3 * nh):
        sl = slice(col * HEAD_DIM, (col + 1) * HEAD_DIM)
        acc = xc_ref[8:, sl] * w_ref[GDN_CONV - 1:GDN_CONV, sl]
        for j in range(GDN_CONV - 1):
            acc = acc + xc_ref[pl.ds(8 - (GDN_CONV - 1) + j, tb), sl] * w_ref[j:j + 1, sl]
        y = acc * jax.nn.sigmoid(acc)
        if col < 2 * nh:
            y = y * lax.rsqrt(jnp.sum(y * y, axis=-1, keepdims=True) + EPS)
            if col < nh:
                y = y * HEAD_DIM ** -0.5
        y_ref[0, :, sl] = y


def _gdn_prep(h3, conv_buf, conv_w, lay):
    b, t, _ = h3.shape
    w = 3 * lay.mix
    tb = _pick_tile(t, GDN_PREP_ROWS, 8)
    cb = jnp.pad(conv_buf, ((0, 0), (8 - conv_buf.shape[1], 0), (0, 0)))
    return pl.pallas_call(
        functools.partial(_gdn_prep_body, tb=tb, nh=lay.nh),
        grid=(b, t // tb),
        in_specs=[pl.BlockSpec((1, tb, w), lambda i, n: (i, n, _col_block(lay.off[0], w))),
                  pl.BlockSpec((1, 8, w), lambda i, n: (i, jnp.maximum(n * (tb // 8) - 1, 0), _col_block(lay.off[0], w))),
                  pl.BlockSpec((1, 8, w), lambda i, n: (i, 0, 0)),
                  pl.BlockSpec((GDN_CONV, w), lambda i, n: (0, 0))],
        out_specs=pl.BlockSpec((1, tb, w), lambda i, n: (i, n, 0)),
        out_shape=jax.ShapeDtypeStruct((b, t, w), F32),
        scratch_shapes=[pltpu.VMEM((tb + 8, w), F32)],
        compiler_params=pltpu.CompilerParams(dimension_semantics=("parallel", "parallel"),
                                             vmem_limit_bytes=VMEM_LIMIT),
        name="gdn_prep",
    )(h3, h3, cb, conv_w)


def _split3(x):
    hi = x.astype(BF16)
    rem = x - hi.astype(F32)
    mid = rem.astype(BF16)
    return hi, mid, (rem - mid.astype(F32)).astype(BF16)


def _mm_hi(x, y):
    xh, xl, _ = _split3(x)
    yh, yl, _ = _split3(y)
    dot = lambda a, b: jnp.dot(a, b, preferred_element_type=F32)
    return dot(xh, yh) + dot(xh, yl) + dot(xl, yh)


def _gdn_body(q_ref, k_ref, v_ref, z_ref, sm_ref, par_ref, ng_ref, s0_ref, o_ref, sout_ref, s_ref, *, c, nh, n_chunks):
    n = pl.program_id(1)

    @pl.when(n == 0)
    def _():
        s_ref[...] = s0_ref[0]

    sm = sm_ref[0]
    x = sm + par_ref[1:2, :]
    softplus = jnp.maximum(x, 0.0) + jnp.log(1.0 + jnp.exp(-jnp.abs(x)))
    gc = -jnp.exp(par_ref[0:1, :]) * softplus
    beta_all = jax.nn.sigmoid(sm)
    row = _iota((c, LANE), 0)
    step = 1
    while step < c:
        gc = gc + jnp.where(row >= step, pltpu.roll(gc, step, 0), 0.0)
        step *= 2
    gct = gc.T
    ii, jj = _iota((c, c), 0), _iota((c, c), 1)
    incl, strict = ii >= jj, ii > jj
    eye = jnp.where(ii == jj, 1.0, 0.0)
    dot = lambda a, b: jnp.dot(a, b, preferred_element_type=F32)
    heads = range(nh)
    sls = [slice(h * HEAD_DIM, (h + 1) * HEAD_DIM) for h in heads]
    gcol = [gc[:, nh + h:nh + h + 1] for h in heads]
    glast = [gc[c - 1:c, nh + h:nh + h + 1] for h in heads]
    beta = [beta_all[:, h:h + 1] for h in heads]
    a, qk, rhs, lhs_q, kd_t = [], [], [], [], []
    for h in heads:
        decay = jnp.where(incl, jnp.exp(jnp.where(incl, gcol[h] - gct[nh + h:nh + h + 1, :], 0.0)), 0.0)
        q, k, v = q_ref[0, :, sls[h]], k_ref[0, :, sls[h]], v_ref[0, :, sls[h]]
        kb = k * beta[h]
        eg = jnp.exp(gcol[h])
        kk = lax.dot_general(jnp.concatenate([kb, q], axis=0).astype(BF16), k.astype(BF16), _NT,
                             preferred_element_type=F32)
        a.append(jnp.where(strict, kk[:c] * decay, 0.0))
        qk.append((kk[c:] * decay).astype(BF16))
        rhs.append(jnp.concatenate([v * beta[h], kb * eg], axis=1).astype(BF16))
        lhs_q.append((q * eg).astype(BF16))
        kd_t.append((k * jnp.exp(glast[h] - gcol[h])).T.astype(BF16))
    t_inv, p = [eye - a[h] for h in heads], a
    power = 2
    while power < c:
        p = [_mm_hi(p[h], p[h]) for h in heads]
        t_inv = [t_inv[h] + _mm_hi(t_inv[h], p[h]) for h in heads]
        power *= 2
    uw = [dot(t_inv[h].astype(BF16), rhs[h]) for h in heads]
    ws = [dot(jnp.concatenate([uw[h][:, HEAD_DIM:].astype(BF16), lhs_q[h]], axis=0), s_ref[h].astype(BF16))
          for h in heads]
    v_new = [(uw[h][:, :HEAD_DIM] - ws[h][:c]).astype(BF16) for h in heads]
    for h in heads:
        s_ref[h] = s_ref[h] * jnp.exp(glast[h]) + dot(kd_t[h], v_new[h])
    for h in heads:
        o = ws[h][c:] + dot(qk[h], v_new[h])
        o = o * lax.rsqrt(jnp.mean(o * o, axis=-1, keepdims=True) + EPS) * ng_ref[...]
        z = z_ref[0, :, sls[h]]
        o_ref[0, :, sls[h]] = (o * (z * jax.nn.sigmoid(z))).astype(o_ref.dtype)

    @pl.when(n == n_chunks - 1)
    def _():
        sout_ref[0] = s_ref[...]


def _gdn_chunks(y, h3, s0, a_log, dt_bias, norm_g, lay):
    b, t, _ = y.shape
    nh, mix = lay.nh, lay.mix
    c = GDN_CHUNK
    n_chunks = t // c
    par = jnp.zeros((2, LANE), F32).at[0, nh:2 * nh].set(a_log).at[1, nh:2 * nh].set(dt_bias)
    assert lay.off[3] - lay.off[2] == nh
    tok = lambda off, width: pl.BlockSpec((1, c, width), lambda i, n: (i, n, _col_block(off, width)))
    state = pl.BlockSpec((1, nh, HEAD_DIM, HEAD_DIM), lambda i, n: (i, 0, 0, 0))
    return pl.pallas_call(
        functools.partial(_gdn_body, c=c, nh=nh, n_chunks=n_chunks),
        grid=(b, n_chunks),
        in_specs=[tok(0, mix), tok(mix, mix), tok(2 * mix, mix), tok(lay.off[1], mix), tok(lay.off[2], LANE),
                  pl.BlockSpec((2, LANE), lambda i, n: (0, 0)), pl.BlockSpec((1, HEAD_DIM), lambda i, n: (0, 0)),
                  state],
        out_specs=[pl.BlockSpec((1, c, mix), lambda i, n: (i, n, 0)), state],
        out_shape=[jax.ShapeDtypeStruct((b, t, mix), BF16), jax.ShapeDtypeStruct(s0.shape, F32)],
        scratch_shapes=[pltpu.VMEM((nh, HEAD_DIM, HEAD_DIM), F32)],
        compiler_params=pltpu.CompilerParams(dimension_semantics=("parallel", "arbitrary"),
                                             vmem_limit_bytes=VMEM_LIMIT),
        name="gdn_chunks",
    )(y, y, y, h3, h3, par, norm_g.reshape(1, HEAD_DIM), s0)


def _ret_body(q_ref, k_ref, v_ref, g_ref, cos_ref, sin_ref, gn_ref, s0_ref, o_ref, sout_ref, s_ref, *, c, nh, n_chunks):
    n = pl.program_id(1)

    @pl.when(n == 0)
    def _():
        s_ref[...] = s0_ref[0]

    cos, sin = cos_ref[...], sin_ref[...]
    ii, jj = _iota((c, c), 0), _iota((c, c), 1)
    lower = ii >= jj
    dist = jnp.where(lower, ii - jj, 0).astype(F32)
    icol = _iota((c, 1), 0).astype(F32)
    dot = lambda a, b: jnp.dot(a, b, preferred_element_type=F32)
    rope = lambda x: x * cos + pltpu.roll(x, HEAD_DIM // 2, 1) * sin
    for h in range(nh):
        lg = math.log1p(-2.0 ** (-5.0 - h))
        sl = slice(h * HEAD_DIM, (h + 1) * HEAD_DIM)
        q = rope(q_ref[0, :, sl])
        k = rope(k_ref[0, :, sl]) * HEAD_DIM ** -0.5
        v_bf = v_ref[0, :, sl].astype(BF16)
        qk = lax.dot_general(q.astype(BF16), k.astype(BF16), _NT, preferred_element_type=F32)
        qk = qk * jnp.where(lower, jnp.exp(dist * lg), 0.0)
        s_bf = s_ref[h].astype(BF16)
        o = dot(qk.astype(BF16), v_bf) + dot((q * jnp.exp((icol + 1.0) * lg)).astype(BF16), s_bf)
        k_dec = k * jnp.exp((c - 1.0 - icol) * lg)
        s_ref[h] = s_ref[h] * math.exp(c * lg) + dot(k_dec.T.astype(BF16), v_bf)
        oc = o - jnp.mean(o, axis=-1, keepdims=True)
        o = oc * lax.rsqrt(jnp.mean(oc * oc, axis=-1, keepdims=True) + EPS)
        gate = g_ref[0, :, sl]
        o_ref[0, :, sl] = (gate * jax.nn.sigmoid(gate) * (o * gn_ref[:, sl])).astype(o_ref.dtype)

    @pl.when(n == n_chunks - 1)
    def _():
        sout_ref[0] = s_ref[...]


def _ret_chunks(h3, pos, s0, gn_g, lay):
    b, t, _ = h3.shape
    nh, mix = lay.nh, lay.mix
    c = _pick_tile(t, RET_KERNEL_CHUNK, 8)
    n_chunks = t // c
    half = HEAD_DIM // 2
    freq = jnp.exp(-math.log(ROPE_BASE) * jnp.arange(half, dtype=F32) / half)
    ang = pos.astype(F32)[:, None] * freq
    cos, sin = jnp.cos(ang), jnp.sin(ang)
    cos2, sin2 = jnp.concatenate([cos, cos], axis=-1), jnp.concatenate([-sin, sin], axis=-1)
    tok = lambda off: pl.BlockSpec((1, c, mix), lambda i, n: (i, n, _col_block(off, mix)))
    table = pl.BlockSpec((c, HEAD_DIM), lambda i, n: (n, 0))
    state = pl.BlockSpec((1, nh, HEAD_DIM, HEAD_DIM), lambda i, n: (i, 0, 0, 0))
    return pl.pallas_call(
        functools.partial(_ret_body, c=c, nh=nh, n_chunks=n_chunks),
        grid=(b, n_chunks),
        in_specs=[tok(lay.off[4]), tok(lay.off[5]), tok(lay.off[6]), tok(lay.off[7]), table, table,
                  pl.BlockSpec((1, mix), lambda i, n: (0, 0)), state],
        out_specs=[pl.BlockSpec((1, c, mix), lambda i, n: (i, n, 0)), state],
        out_shape=[jax.ShapeDtypeStruct((b, t, mix), BF16), jax.ShapeDtypeStruct(s0.shape, F32)],
        scratch_shapes=[pltpu.VMEM((nh, HEAD_DIM, HEAD_DIM), F32)],
        compiler_params=pltpu.CompilerParams(dimension_semantics=("parallel", "arbitrary"),
                                             vmem_limit_bytes=VMEM_LIMIT),
        name="retention_chunks",
    )(h3, h3, h3, h3, cos2, sin2, gn_g.reshape(1, mix), s0)


INT_MIN = -2 ** 31
PAD_SCORE = -3e38


def _sortable(x):
    b = pltpu.bitcast(x, jnp.int32)
    return b ^ ((b >> 31) & 0x7FFFFFFF)


def _kth_largest(key_ref, nc, k, n_live, two_bits):
    def count_ge(t):
        if n_live is None:
            acc = jnp.where(key_ref[0] >= t, 1.0, 0.0)
            for c in range(1, nc):
                acc = acc + jnp.where(key_ref[c] >= t, 1.0, 0.0)
        else:
            acc = lax.fori_loop(0, n_live, lambda c, acc: acc + jnp.where(key_ref[c] >= t, 1.0, 0.0),
                                jnp.zeros(key_ref.shape[1:], F32))
        return jnp.sum(acc, axis=-1, keepdims=True)

    tq = key_ref.shape[1]
    zero = jnp.zeros((tq, 1), jnp.int32)
    base = jnp.where(count_ge(zero) >= k, zero, jnp.full((tq, 1), INT_MIN, jnp.int32))

    def one_bit(bit, base):
        cand = base | (jnp.int32(1) << bit)
        return jnp.where(count_ge(cand) >= k, cand, base)

    if not two_bits:
        return lax.fori_loop(0, 31, lambda i, base: one_bit(30 - i, base), base)

    def two_bit_step(i, base):
        lo = 29 - 2 * i
        c1, c2, c3 = (base | (jnp.int32(v) << lo) for v in (1, 2, 3))
        n1, n2, n3 = count_ge(c1), count_ge(c2), count_ge(c3)
        return jnp.where(n3 >= k, c3, jnp.where(n2 >= k, c2, jnp.where(n1 >= k, c1, base)))

    return one_bit(0, lax.fori_loop(0, 15, two_bit_step, base))


def _topk_blocks(key_ref, nc, k, n_live=None, two_bits=False):
    _, tq, lc = key_ref.shape
    thr = _kth_largest(key_ref, nc, k, n_live, two_bits)
    n_gt = jnp.where(key_ref[0] > thr, 1.0, 0.0)
    for c in range(1, nc):
        n_gt = n_gt + jnp.where(key_ref[c] > thr, 1.0, 0.0)
    need = k - jnp.sum(n_gt, axis=-1, keepdims=True)
    tri = jnp.where(_iota((LANE, LANE), 0) <= _iota((LANE, LANE), 1), 1.0, 0.0).astype(BF16)
    carry = jnp.zeros((tq, 1), F32)
    for c in range(nc):
        for j in range(lc // LANE):
            kk = key_ref[c, :, j * LANE:(j + 1) * LANE]
            eq = kk == thr
            eqf = jnp.where(eq, 1.0, 0.0)
            prefix = jnp.dot(eqf.astype(BF16), tri, preferred_element_type=F32) + carry
            carry = carry + jnp.sum(eqf, axis=-1, keepdims=True)
            yield c, j, (kk > thr) | (eq & (prefix <= need))


def _dsa_select_body(qi_ref, sm_ref, ki_ref, o_ref, qh32_ref, qh_ref, d_ref, key_ref, *, tq, lc, nc, pos0, topk, wi_off):
    q_first = pos0 + pl.program_id(1) * tq
    qpos = q_first + _iota((tq, 1), 0)
    n_live = jnp.minimum(nc, (q_first + tq - 1) // lc + 1)
    qi = qi_ref[0]
    for h in range(IDX_HEADS):
        qh32_ref[h * tq:(h + 1) * tq, :] = qi[:, h * IDX_DIM:(h + 1) * IDX_DIM]
    qh_ref[...] = qh32_ref[...].astype(BF16)
    sm = sm_ref[0]
    for c in range(nc):
        @pl.when(c < n_live)
        def _():
            kic = ki_ref[0, c * lc:(c + 1) * lc, :IDX_DIM].astype(BF16)
            d_ref[...] = lax.dot_general(qh_ref[...], kic, _NT, preferred_element_type=F32)
            score = jnp.zeros((tq, lc), F32)
            for h in range(IDX_HEADS):
                score = score + sm[:, wi_off + h:wi_off + h + 1] * jnp.maximum(d_ref[h * tq:(h + 1) * tq, :], 0.0)
            kpos = c * lc + _iota((tq, lc), 1)
            key_ref[c] = _sortable(jnp.where(kpos <= qpos, score, NEG))

        @pl.when(c >= n_live)
        def _():
            key_ref[c] = _sortable(jnp.full((tq, lc), NEG, F32))
    pieces = []
    for c, j, sel in _topk_blocks(key_ref, nc, topk, n_live):
        kpos = c * lc + j * LANE + _iota((tq, LANE), 1)
        pieces.append(jnp.where(sel & (kpos <= qpos), 1.0, 0.0))
        if len(pieces) == lc // LANE:
            o_ref[0, c] = jnp.concatenate(pieces, axis=1).astype(BF16)
            pieces = []


def _dsa_select(hq, qi_off, sm_off, rows, ki_off, *, tq, lc, pos0, l_real, wi_off):
    b, t, _ = hq.shape
    l = rows.shape[1]
    nc = l // lc
    topk = min(DSA_TOPK_MAX, l_real // 4)
    qw = IDX_HEADS * IDX_DIM
    return pl.pallas_call(
        functools.partial(_dsa_select_body, tq=tq, lc=lc, nc=nc, pos0=pos0, topk=topk, wi_off=wi_off),
        grid=(b, t // tq),
        in_specs=[pl.BlockSpec((1, tq, qw), lambda i, q: (i, q, _col_block(qi_off, qw))),
                  pl.BlockSpec((1, tq, LANE), lambda i, q: (i, q, _col_block(sm_off, LANE))),
                  pl.BlockSpec((1, l, LANE), lambda i, q: (i, 0, _col_block(ki_off, LANE)))],
        out_specs=pl.BlockSpec((1, nc, tq, lc), lambda i, q: (i, 0, q, 0)),
        out_shape=jax.ShapeDtypeStruct((b, nc, t, lc), BF16),
        scratch_shapes=[pltpu.VMEM((IDX_HEADS * tq, IDX_DIM), F32), pltpu.VMEM((IDX_HEADS * tq, IDX_DIM), BF16),
                        pltpu.VMEM((IDX_HEADS * tq, lc), F32), pltpu.VMEM((nc, tq, lc), jnp.int32)],
        compiler_params=pltpu.CompilerParams(dimension_semantics=("parallel", "arbitrary"),
                                             vmem_limit_bytes=VMEM_LIMIT),
        name="dsa_select",
    )(hq, hq, rows)


def _nsa_select_body(q_ref, kc0_ref, kc1_ref, vc0_ref, vc1_ref, wk_ref, wv_ref, bk_ref, bv_ref, ov_ref, e_ref,
                     ocmp_ref, mask_ref, ck_ref, cv_ref, cacc_ref, key_ref, *, tq, lc, nc, pos0, l16, ns, n_sel, grp):
    qb = pl.program_id(1)

    @pl.when(qb == 0)
    def _():
        for srcs, w_ref, b_ref, dst in (((kc0_ref, kc1_ref), wk_ref, bk_ref, ck_ref),
                                        ((vc0_ref, vc1_ref), wv_ref, bv_ref, cv_ref)):
            for g in range(KV_HEADS):
                for c in range(CMP_STRIDE):
                    x = srcs[g][0, pl.ds(c, l16, stride=CMP_STRIDE), :].astype(BF16)
                    part = jnp.dot(x, w_ref[c], preferred_element_type=F32)
                    if c == 0:
                        cacc_ref[...] = part
                    else:
                        cacc_ref[...] += part
                acc = cacc_ref[...]
                summ = acc[:, :HEAD_DIM] + pltpu.roll(acc[:, HEAD_DIM:], l16 - 1, 0) + b_ref[...]
                dst[g] = summ.astype(BF16)

    qpos = pos0 + qb * tq + _iota((tq, 1), 0)
    vis = (_iota((tq, l16), 1) * CMP_STRIDE + (CMP_BLOCK - 1)) <= qpos
    jcol = _iota((tq, ns), 1)
    cur = qpos >> int(math.log2(SEL_BLOCK))
    forced = (jcol == 0) | (jcol == cur) | (jcol == cur - 1)
    admissible = jcol * SEL_BLOCK <= qpos
    scale = HEAD_DIM ** -0.5
    for g in range(KV_HEADS):
        imp_c = jnp.zeros((tq, l16), F32)
        for r in range(grp):
            hd = g * grp + r
            q = q_ref[0][:, hd * HEAD_DIM:(hd + 1) * HEAD_DIM].astype(BF16)
            s = lax.dot_general(q, ck_ref[g], _NT, preferred_element_type=F32) * scale
            s = jnp.where(vis, s, NEG)
            e = jnp.exp(s - jnp.max(s, axis=-1, keepdims=True))
            p = jnp.where(vis, e / jnp.sum(e, axis=-1, keepdims=True), 0.0)
            imp_c = imp_c + p
            ocmp_ref[0, :, hd * HEAD_DIM:(hd + 1) * HEAD_DIM] = jnp.dot(
                p.astype(BF16), cv_ref[g], preferred_element_type=F32)
        hi = imp_c.astype(BF16)
        rem = imp_c - hi.astype(F32)
        mid = rem.astype(BF16)
        lo = (rem - mid.astype(F32)).astype(BF16)
        ov = ov_ref[...]
        imp = (jnp.dot(hi, ov, preferred_element_type=F32) + jnp.dot(mid, ov, preferred_element_type=F32)
               + jnp.dot(lo, ov, preferred_element_type=F32))
        imp = jnp.where(admissible, jnp.where(forced, imp + FORCE_BONUS, imp), NEG)
        imp = jnp.where(jcol < n_sel, imp, PAD_SCORE)
        key_ref[0, g * tq:(g + 1) * tq, :] = _sortable(imp)
    sel_all = jnp.concatenate(
        [jnp.where(s_, 1.0, 0.0) for _, _, s_ in _topk_blocks(key_ref, 1, min(SEL_TOPN, n_sel), two_bits=True)],
        axis=1)
    for g in range(KV_HEADS):
        sel = sel_all[g * tq:(g + 1) * tq].astype(BF16)
        for c in range(nc):
            tok = jnp.dot(sel, e_ref[:, c * lc:(c + 1) * lc], preferred_element_type=F32)
            kpos = c * lc + _iota((tq, lc), 1)
            mask_ref[g, c] = jnp.where((tok > 0.5) & (kpos <= qpos), 1.0, 0.0).astype(BF16)


def _nsa_select(hq, q_off, rows, kc_off, vc_off, cwk, cwv, pek, pev, *, tq, lc, pos0, l_real, nh):
    b, t, _ = hq.shape
    l = rows.shape[1]
    nc = l // lc
    grp = nh // KV_HEADS
    n_cmp = (l_real - CMP_BLOCK) // CMP_STRIDE + 1
    l16 = n_cmp + 1
    n_sel = -(-l_real // SEL_BLOCK)
    ns = _round_up(n_sel, LANE)
    pack = lambda w: jnp.concatenate([w[:CMP_STRIDE], w[CMP_STRIDE:]], axis=-1).astype(BF16)
    bias = lambda pe, w: jnp.einsum('cd,cde->e', pe, w, precision=lax.Precision.HIGHEST).reshape(1, HEAD_DIM)
    c0 = np.arange(l16)[:, None] * CMP_STRIDE
    s0 = np.arange(ns)[None, :] * SEL_BLOCK
    overlap = jnp.asarray((c0 < s0 + SEL_BLOCK) & (c0 + CMP_BLOCK > s0), BF16)
    expand = jnp.asarray(np.arange(l)[None, :] // SEL_BLOCK == np.arange(ns)[:, None], BF16)
    qw = nh * HEAD_DIM
    lrows = CMP_STRIDE * l16
    const = lambda shape: pl.BlockSpec(shape, lambda i, q: (0,) * len(shape))
    head_rows = lambda off, g: pl.BlockSpec((1, lrows, HEAD_DIM), lambda i, q: (i, 0, _col_block(off, HEAD_DIM) + g))
    assert KV_HEADS == 2
    return pl.pallas_call(
        functools.partial(_nsa_select_body, tq=tq, lc=lc, nc=nc, pos0=pos0, l16=l16, ns=ns, n_sel=n_sel, grp=grp),
        grid=(b, t // tq),
        in_specs=[pl.BlockSpec((1, tq, qw), lambda i, q: (i, q, _col_block(q_off, qw))),
                  head_rows(kc_off, 0), head_rows(kc_off, 1), head_rows(vc_off, 0), head_rows(vc_off, 1),
                  const((CMP_STRIDE, HEAD_DIM, 2 * HEAD_DIM)), const((CMP_STRIDE, HEAD_DIM, 2 * HEAD_DIM)),
                  const((1, HEAD_DIM)), const((1, HEAD_DIM)), const((l16, ns)), const((ns, l))],
        out_specs=[pl.BlockSpec((1, tq, qw), lambda i, q: (i, q, 0)),
                   pl.BlockSpec((KV_HEADS, nc, tq, lc), lambda i, q: (i, 0, q, 0))],
        out_shape=[jax.ShapeDtypeStruct((b, t, qw), F32),
                   jax.ShapeDtypeStruct((b * KV_HEADS, nc, t, lc), BF16)],
        scratch_shapes=[pltpu.VMEM((KV_HEADS, l16, HEAD_DIM), BF16), pltpu.VMEM((KV_HEADS, l16, HEAD_DIM), BF16),
                        pltpu.VMEM((l16, 2 * HEAD_DIM), F32), pltpu.VMEM((1, KV_HEADS * tq, ns), jnp.int32)],
        compiler_params=pltpu.CompilerParams(dimension_semantics=("parallel", "arbitrary"),
                                             vmem_limit_bytes=VMEM_LIMIT),
        name="nsa_select",
    )(hq, rows, rows, rows, rows, pack(cwk), pack(cwv), bias(pek, cwk), bias(pev, cwv), overlap, expand)


def _attend_body(*refs, banded, tq, lc, nc, pos0, kpos0, grp):
    if banded:
        q_ref, k_ref, v_ref, o_ref, s_ref, acc_ref = refs
    else:
        q_ref, k_ref, v_ref, m_ref, o_ref, s_ref, acc_ref = refs
    q_first = pos0 + pl.program_id(2) * tq
    qpos = q_first + _iota((tq, 1), 0)
    c_hi = jnp.minimum(nc, (q_first + tq - 1 - kpos0) // lc + 1)
    c_lo = jnp.maximum(q_first - WINDOW - kpos0, 0) // lc if banded else 0
    scale = HEAD_DIM ** -0.5
    qs = [q_ref[0][:, r * HEAD_DIM:(r + 1) * HEAD_DIM].astype(BF16) for r in range(grp)]

    unroll = 2 if nc % 2 == 0 else 1
    g_lo, g_hi = c_lo // unroll, (c_hi + unroll - 1) // unroll

    def scores(g, ms):
        ms = list(ms)
        for u in range(unroll):
            c = g * unroll + u
            kc = k_ref[0, pl.ds(pl.multiple_of(c * lc, lc), lc), :].astype(BF16)
            if banded:
                kpos = kpos0 + c * lc + _iota((tq, lc), 1)
                valid = (kpos <= qpos) & (qpos - kpos <= WINDOW)
            else:
                valid = m_ref[0, c].astype(F32) > 0.5
            for r in range(grp):
                s = jnp.where(valid, lax.dot_general(qs[r], kc, _NT, preferred_element_type=F32) * scale, NEG)
                s_ref[r, c] = s
                ms[r] = jnp.maximum(ms[r], jnp.max(s, axis=-1, keepdims=True))
        return tuple(ms)

    ms = lax.fori_loop(g_lo, g_hi, scores, tuple(jnp.full((tq, 1), NEG, F32) for _ in range(grp)))
    acc_ref[...] = jnp.zeros_like(acc_ref)

    def values(g, ls):
        ls = list(ls)
        cs = [g * unroll + u for u in range(unroll)]
        vcs = [v_ref[0, pl.ds(pl.multiple_of(c * lc, lc), lc), :].astype(BF16) for c in cs]
        for r in range(grp):
            pv = None
            for c, vc in zip(cs, vcs):
                p = jnp.exp(s_ref[r, c] - ms[r])
                d = jnp.dot(p.astype(BF16), vc, preferred_element_type=F32)
                pv = d if pv is None else pv + d
                ls[r] = ls[r] + jnp.sum(p, axis=-1, keepdims=True)
            acc_ref[r] += pv
        return tuple(ls)

    ls = lax.fori_loop(g_lo, g_hi, values, tuple(jnp.zeros((tq, 1), F32) for _ in range(grp)))
    for r in range(grp):
        o_ref[0, :, r * HEAD_DIM:(r + 1) * HEAD_DIM] = (acc_ref[r] / ls[r]).astype(o_ref.dtype)


def _attend(hq, q_off, kv, k_off, v_off, mask, *, tq, lc, pos0, kpos0, nh, mask_per_group=False, out_dtype=F32,
            name="attend"):
    b, t, _ = hq.shape
    l = kv.shape[1]
    nc = l // lc
    grp = nh // KV_HEADS
    qw = grp * HEAD_DIM
    banded = mask is None
    in_specs = [pl.BlockSpec((1, tq, qw), lambda i, g, q: (i, q, _col_block(q_off, qw) + g)),
                pl.BlockSpec((1, l, HEAD_DIM), lambda i, g, q: (i, 0, _col_block(k_off, HEAD_DIM) + g)),
                pl.BlockSpec((1, l, HEAD_DIM), lambda i, g, q: (i, 0, _col_block(v_off, HEAD_DIM) + g))]
    args = [hq, kv, kv]
    if not banded:
        if mask_per_group:
            in_specs.append(pl.BlockSpec((1, nc, tq, lc), lambda i, g, q: (i * KV_HEADS + g, 0, q, 0)))
        else:
            in_specs.append(pl.BlockSpec((1, nc, tq, lc), lambda i, g, q: (i, 0, q, 0)))
        args.append(mask)
    return pl.pallas_call(
        functools.partial(_attend_body, banded=banded, tq=tq, lc=lc, nc=nc, pos0=pos0, kpos0=kpos0, grp=grp),
        grid=(b, KV_HEADS, t // tq),
        in_specs=in_specs,
        out_specs=pl.BlockSpec((1, tq, qw), lambda i, g, q: (i, q, g)),
        out_shape=jax.ShapeDtypeStruct((b, t, nh * HEAD_DIM), out_dtype),
        scratch_shapes=[pltpu.VMEM((grp, nc, tq, lc), F32), pltpu.VMEM((grp, tq, HEAD_DIM), F32)],
        compiler_params=pltpu.CompilerParams(dimension_semantics=("parallel", "parallel", "arbitrary"),
                                             vmem_limit_bytes=VMEM_LIMIT),
        name=name,
    )(*args)


def _nsa_combine_body(a_ref, b_ref, c_ref, sm_ref, o_ref, *, gate_off, nh):
    gate = jax.nn.sigmoid(sm_ref[...])
    for hd in range(nh):
        sl = slice(hd * HEAD_DIM, (hd + 1) * HEAD_DIM)
        col = lambda br: gate[:, gate_off + br * nh + hd:gate_off + br * nh + hd + 1]
        o = col(0) * a_ref[:, sl] + col(1) * b_ref[:, sl] + col(2) * c_ref[:, sl]
        o_ref[:, sl] = o.astype(o_ref.dtype)


def _nsa_combine(o_cmp, o_sel, o_win, h, sm_off, gate_off, nh):
    m, w = o_cmp.shape
    tm = _pick_tile(m, 512, 8)
    blk = pl.BlockSpec((tm, w), lambda i: (i, 0))
    return pl.pallas_call(
        functools.partial(_nsa_combine_body, gate_off=gate_off, nh=nh),
        grid=(m // tm,),
        in_specs=[blk, blk, blk, pl.BlockSpec((tm, LANE), lambda i: (i, _col_block(sm_off, LANE)))],
        out_specs=blk,
        out_shape=jax.ShapeDtypeStruct((m, w), BF16),
        compiler_params=pltpu.CompilerParams(dimension_semantics=("parallel",), vmem_limit_bytes=VMEM_LIMIT),
        name="nsa_combine",
    )(o_cmp, o_sel, o_win, h)


def _sparse_mixers(h, hq, kv, kv_off, win, win_off, win_pos0, lay, cw, *, tq, pos0, l_real, t_real):
    cwk, cwv, pek, pev = cw
    nh = lay.nh
    b = hq.shape[0]
    lc = _pick_tile(kv.shape[1], 32 * 1024 // tq, LANE)
    lcw = _pick_tile(win.shape[1], 16 * 1024 // tq, LANE)
    sm_off = lay.off[2]
    offs = np.concatenate([[0], np.cumsum(PAGED_SIZES)]).tolist()
    common = dict(tq=tq, pos0=pos0, nh=nh)
    o_cmp, sel_mask = _nsa_select(hq, lay.off[8], kv, kv_off + offs[0], kv_off + offs[1], cwk, cwv, pek, pev,
                                  lc=lc, l_real=l_real, **common)
    o_sel = _attend(hq, lay.off[8], kv, kv_off + offs[2], kv_off + offs[3], sel_mask, lc=lc, kpos0=0,
                    mask_per_group=True, name="nsa_selected", **common)
    o_win = _attend(hq, lay.off[8], win, win_off, win_off + KVW, None, lc=lcw, kpos0=win_pos0,
                    name="nsa_window", **common)
    dsa_mask = _dsa_select(hq, lay.off[19], sm_off, kv, kv_off + offs[6], tq=tq, lc=lc, pos0=pos0, l_real=l_real,
                           wi_off=lay.off[21] - sm_off)
    o_d = _attend(hq, lay.off[16], kv, kv_off + offs[4], kv_off + offs[5], dsa_mask, lc=lc, kpos0=0,
                  out_dtype=BF16, name="dsa_attend", **common)
    unpad = lambda o: o[:, :t_real].reshape(b * t_real, nh * HEAD_DIM)
    o_c = _nsa_combine(unpad(o_cmp), unpad(o_sel), unpad(o_win), h, sm_off, lay.off[15] - sm_off, nh)
    return o_c, unpad(o_d)


def _gather_body(pt_ref, x_ref, *o_refs, n_pages):
    del pt_ref
    live = pl.program_id(1) < n_pages

    @pl.when(live)
    def _():
        for l, o_ref in enumerate(o_refs):
            for c0 in range(0, PAGED_W, LANE):
                cw = min(LANE, PAGED_W - c0)
                o_ref[0, :, c0:c0 + cw] = x_ref[0, l, c0:c0 + cw, :].T
            o_ref[0, :, PAGED_W:] = jnp.zeros((o_ref.shape[1], o_ref.shape[2] - PAGED_W), F32)

    @pl.when(jnp.logical_not(live))
    def _():
        for o_ref in o_refs:
            o_ref[...] = jnp.zeros(o_ref.shape, F32)


def _paged_gather(cache_kv, page_table, l_pad):
    _, page, depth, width = cache_kv.shape
    b, n_pages = page_table.shape
    wp = _round_up(width, LANE)
    cache_t = jnp.transpose(cache_kv, (0, 2, 3, 1))
    out = pl.BlockSpec((1, page, wp), lambda i, p, pt: (i, p, 0))
    return pl.pallas_call(
        functools.partial(_gather_body, n_pages=n_pages),
        grid_spec=pltpu.PrefetchScalarGridSpec(
            num_scalar_prefetch=1,
            grid=(b, l_pad // page),
            in_specs=[pl.BlockSpec((1, depth, width, page),
                                   lambda i, p, pt: (pt[i * n_pages + jnp.minimum(p, n_pages - 1)], 0, 0, 0))],
            out_specs=[out] * depth),
        out_shape=[jax.ShapeDtypeStruct((b, l_pad, wp), F32)] * depth,
        compiler_params=pltpu.CompilerParams(dimension_semantics=("parallel", "arbitrary"),
                                             vmem_limit_bytes=VMEM_LIMIT),
        name="paged_gather",
    )(page_table.reshape(-1), cache_t)


def _layer(x, q_pos, lw, layer, past, lay):
    (g_mix, g_mlp, w_in_p, conv_w, a_log, dt_bias, gdn_g, ret_g, cwk, cwv, pek, pev, w_br, w_o, w_up_l, w_down_l) = lw
    conv_buf, s_gdn, s_ret, win_buf, past_kv, past_len = past
    b, t, d = x.shape
    m = b * t
    mix, nh = lay.mix, lay.nh
    x2d = x.reshape(m, d)
    h = _matmul(_rmsnorm(x2d, g_mix, BF16), w_in_p, layer, name="in_proj")
    h3 = h.reshape(b, t, lay.width)
    seg = lambda idx: h3[:, :, lay.off[idx]:lay.off[idx] + lay.in_sizes[idx]]

    if t % GDN_CHUNK == 0:
        o_a, s_gdn_new = _gdn_chunks(_gdn_prep(h3, conv_buf, conv_w, lay), h3, s_gdn, a_log, dt_bias, gdn_g, lay)
        conv_new = jnp.concatenate([conv_buf, seg(0)[:, t - (GDN_CONV - 1):]], axis=1)[:, -(GDN_CONV - 1):]
        o_b, s_ret_new = _ret_chunks(h3, q_pos, s_ret, ret_g, lay)
    else:
        o_a, conv_new, s_gdn_new = _gdn(seg(0), seg(1), seg(2), seg(3), conv_buf, s_gdn, conv_w, a_log, dt_bias,
                                        gdn_g, nh)
        o_b, s_ret_new = _retention(seg(4), seg(5), seg(6), seg(7), q_pos, s_ret, ret_g, nh)

    new_rows = h3[:, :, lay.off[9]:lay.off[9] + PAGED_W]
    win_rows = h3[:, :, lay.off[13]:lay.off[13] + 2 * KVW]
    cw = (cwk, cwv, pek, pev)
    if past_kv is None:
        win_new = win_rows[:, t - min(WINDOW, t):]
        o_c, o_d = _sparse_mixers(h, h3, h3, lay.off[9], h3, lay.off[13], 0, lay, cw,
                                  tq=Q_BLOCK, pos0=0, l_real=t, t_real=t)
    else:
        wb = win_buf.shape[1]
        tq = _round_up(t, 8)
        pad_rows = lambda a, n: jnp.pad(a, ((0, 0), (0, n - a.shape[1]), (0, 0)))
        lane_pad = past_kv.shape[2] - PAGED_W
        kv = lax.dynamic_update_slice(past_kv, jnp.pad(new_rows, ((0, 0), (0, 0), (0, lane_pad))), (0, past_len, 0))
        win_all = jnp.concatenate([win_buf, win_rows], axis=1)
        win_new = win_all[:, t:]
        o_c, o_d = _sparse_mixers(h, pad_rows(h3, tq), kv, 0, pad_rows(win_all, _round_up(wb + t, LANE)), 0,
                                  past_len - wb, lay, cw, tq=tq, pos0=past_len, l_real=past_len + t, t_real=t)

    branches = [o.reshape(m, mix).astype(BF16) for o in (o_a, o_b, o_c, o_d)]
    mixed = _merge_branches(branches, w_br, layer, h, lay)
    x2 = _matmul(mixed, w_o, layer, epilogue="residual", res=x2d, name="out_proj")
    up = _matmul(_rmsnorm(x2, g_mlp, BF16), w_up_l, layer, epilogue="relu2", out_dtype=BF16, name="ffn_up")
    x3 = _matmul(up, w_down_l, layer, epilogue="residual", res=x2, name="ffn_down")
    return x3.reshape(b, t, d), (new_rows, conv_new, s_gdn_new, s_ret_new, win_new)


def kernel(x_prompt, x_sample, cache_kv, cache_nsa_window, state_gdn, state_gdn_conv, state_retention, page_table, norm_mix, norm_mlp, norm_final, w_in, gdn_conv_w, gdn_a_log, gdn_dt_bias, gdn_norm, ret_norm, nsa_cmp_wk, nsa_cmp_wv, nsa_cmp_pe_k, nsa_cmp_pe_v, w_branch, w_out, w_up, w_down):
    bp, tp, d = x_prompt.shape
    bs, ts, _ = x_sample.shape
    depth = w_in.shape[0]
    lay = _Layout(d)
    past_len = page_table.shape[1] * cache_kv.shape[1]
    pos_p = jnp.arange(tp, dtype=jnp.int32)
    pos_s = past_len + jnp.arange(ts, dtype=jnp.int32)
    zero_conv = jnp.zeros((bp, GDN_CONV - 1, 3 * lay.mix), F32)
    zero_state = jnp.zeros((bp, lay.nh, HEAD_DIM, HEAD_DIM), F32)
    w_in_p = _pack_w_in(w_in, lay)
    w_br_b, w_o_b, w_up_b, w_down_b = (a.astype(BF16) for a in (w_branch, w_out, w_up, w_down))
    past_kv = _paged_gather(cache_kv, page_table, _round_up(past_len + ts, 2 * LANE))
    hp, hs = x_prompt, x_sample
    out_p, out_s = [], []
    for l in range(depth):
        lw = (norm_mix[l], norm_mlp[l], w_in_p, gdn_conv_w[l], gdn_a_log[l], gdn_dt_bias[l], gdn_norm[l],
              ret_norm[l], nsa_cmp_wk[l], nsa_cmp_wv[l], nsa_cmp_pe_k[l], nsa_cmp_pe_v[l], w_br_b,
              w_o_b, w_up_b, w_down_b)
        hp, st_p = _layer(hp, pos_p, lw, l, (zero_conv, zero_state, zero_state, None, None, 0), lay)
        out_p.append(st_p)
        past_s = (state_gdn_conv[:, l], state_gdn[:, l], state_retention[:, l], cache_nsa_window[:, l],
                  past_kv[l], past_len)
        hs, st_s = _layer(hs, pos_s, lw, l, past_s, lay)
        out_s.append(st_s)
    y_prompt = _rmsnorm(hp.reshape(bp * tp, d), norm_final, F32).reshape(bp, tp, d)
    y_sample = _rmsnorm(hs.reshape(bs * ts, d), norm_final, F32).reshape(bs, ts, d)
    stack = lambda outs, j, axis: jnp.stack([o[j] for o in outs], axis=axis)
    return (y_prompt, y_sample,
            stack(out_p, 0, 2), stack(out_s, 0, 2),
            stack(out_p, 2, 1), stack(out_s, 2, 1),
            stack(out_p, 1, 1), stack(out_s, 1, 1),
            stack(out_p, 3, 1), stack(out_s, 3, 1),
            stack(out_p, 4, 1), stack(out_s, 4, 1))
```

```python
import functools
import math

import jax
import jax.numpy as jnp
import numpy as np
from jax import lax
from jax.experimental import pallas as pl
from jax.experimental.pallas import tpu as pltpu

HEAD_DIM = 128
N_BRANCH = 4
EPS = 1e-6
NEG = -1e30
GDN_CONV = 4
GDN_CHUNK = 64
RET_CHUNK = 64
ROPE_BASE = 10000.0
KV_HEADS = 2
CMP_BLOCK = 32
CMP_STRIDE = 16
SEL_BLOCK = 64
SEL_TOPN = 16
WINDOW = 512
FORCE_BONUS = 1e4
IDX_HEADS = 16
IDX_DIM = 64
DSA_TOPK_MAX = 256
Q_BLOCK = 128
KVW = KV_HEADS * HEAD_DIM
PAGED_SIZES = (KVW, KVW, KVW, KVW, KVW, KVW, IDX_DIM)
PAGED_W = sum(PAGED_SIZES)

LANE = 128
MXU_DIM = 256
VMEM_LIMIT = 56 * 1024 * 1024

F32 = jnp.float32
BF16 = jnp.bfloat16


def _round_up(n, m):
    return -(-n // m) * m


def _pick_tile(dim, target, align):
    best = None
    for t in range(align, min(dim, target) + 1, align):
        if dim % t == 0:
            best = t
    return best if best is not None else dim


class _Layout:
    def __init__(self, d_model):
        mix = d_model // N_BRANCH
        nh = mix // HEAD_DIM
        self.d_model, self.mix, self.nh = d_model, mix, nh
        self.in_sizes = (
            3 * mix, mix, nh, nh,
            mix, mix, mix, mix,
            mix, KVW, KVW, KVW, KVW, KVW, KVW, 3 * nh,
            mix, KVW, KVW, IDX_HEADS * IDX_DIM, IDX_DIM, IDX_HEADS,
            N_BRANCH * d_model)
        self.order = [0, 1, 4, 5, 6, 7, 8, 16, 19, 22, 9, 10, 11, 12, 17, 18, 20, None, 13, 14, 2, 3, 15, 21, None]
        self.off = {}
        pos = 0
        self.pads = []
        for idx in self.order:
            if idx is None:
                pad = _round_up(pos, LANE) - pos
                self.pads.append(pad)
                pos += pad
            else:
                self.off[idx] = pos
                pos += self.in_sizes[idx]
        self.width = pos


SUBLANE = 8


def _pack_body(x_ref, o_ref, *, tiles):
    for t, pieces in enumerate(tiles):
        rows = [jnp.zeros((n, LANE), F32) if s is None else x_ref[0, s:s + n, :] for s, n in pieces]
        tile = rows[0] if len(rows) == 1 else jnp.concatenate(rows, axis=0)
        o_ref[0, :, t * LANE:(t + 1) * LANE] = tile.T.astype(BF16)


def _pack_w_in(w_in, lay):
    depth, d, in_width = w_in.shape
    starts = np.concatenate([[0], np.cumsum(lay.in_sizes)]).tolist()
    tiles, pads = [[]], iter(lay.pads)
    for idx in lay.order:
        src, width = (None, next(pads)) if idx is None else (starts[idx], lay.in_sizes[idx])
        while width:
            room = LANE - sum(n for _, n in tiles[-1])
            if room == 0:
                tiles.append([])
                room = LANE
            n = min(room, width)
            assert n % SUBLANE == 0 and (src is None or src % SUBLANE == 0)
            if tiles[-1] and src is not None and tiles[-1][-1][0] is not None \
                    and tiles[-1][-1][0] + tiles[-1][-1][1] == src:
                tiles[-1][-1] = (tiles[-1][-1][0], tiles[-1][-1][1] + n)
            else:
                tiles[-1].append((src, n))
            src = None if src is None else src + n
            width -= n
    assert len(tiles) * LANE == lay.width
    w_t = jnp.swapaxes(w_in, 1, 2)
    return pl.pallas_call(
        functools.partial(_pack_body, tiles=tiles),
        grid=(depth, d // LANE),
        in_specs=[pl.BlockSpec((1, in_width, LANE), lambda l, i: (l, 0, i))],
        out_specs=pl.BlockSpec((1, LANE, lay.width), lambda l, i: (l, i, 0)),
        out_shape=jax.ShapeDtypeStruct((depth, d, lay.width), BF16),
        compiler_params=pltpu.CompilerParams(dimension_semantics=("parallel", "parallel"),
                                             vmem_limit_bytes=VMEM_LIMIT),
        name="pack_w_in",
    )(w_t)


def _rmsnorm_body(x_ref, g_ref, o_ref):
    x = x_ref[...]
    y = x * lax.rsqrt(jnp.mean(x * x, axis=-1, keepdims=True) + EPS)
    o_ref[...] = (y * g_ref[...]).astype(o_ref.dtype)


def _rmsnorm(x, g, out_dtype):
    m, d = x.shape
    tm = _pick_tile(m, 256, 8)
    return pl.pallas_call(
        _rmsnorm_body,
        grid=(m // tm,),
        in_specs=[pl.BlockSpec((tm, d), lambda i: (i, 0)), pl.BlockSpec((1, d), lambda i: (0, 0))],
        out_specs=pl.BlockSpec((tm, d), lambda i: (i, 0)),
        out_shape=jax.ShapeDtypeStruct((m, d), out_dtype),
        compiler_params=pltpu.CompilerParams(dimension_semantics=("parallel",), vmem_limit_bytes=VMEM_LIMIT),
        name="rmsnorm",
    )(x, g.reshape(1, d).astype(F32))


def _mm_body(*refs, nk, epilogue):
    if epilogue == "residual":
        x_ref, w_ref, r_ref, o_ref, acc_ref = refs
    else:
        x_ref, w_ref, o_ref, acc_ref = refs
    k = pl.program_id(2)
    dot = lambda: jnp.dot(x_ref[...], w_ref[...].astype(BF16), preferred_element_type=F32)

    def finish(a):
        if epilogue == "relu2":
            a = jnp.square(jnp.maximum(a, 0.0))
        elif epilogue == "residual":
            a = a + r_ref[...]
        o_ref[...] = a.astype(o_ref.dtype)

    if nk == 1:
        finish(dot())
        return

    @pl.when(k == 0)
    def _():
        acc_ref[...] = dot()

    @pl.when((k > 0) & (k < nk - 1))
    def _():
        acc_ref[...] += dot()

    @pl.when(k == nk - 1)
    def _():
        finish(acc_ref[...] + dot())


def _matmul(x, w, layer, *, epilogue="none", res=None, out_dtype=F32, name="matmul"):
    m, kdim = x.shape
    n = w.shape[2]
    tm = _pick_tile(m, 1024, 8)
    tn = _pick_tile(n, 2304, MXU_DIM) if n % MXU_DIM == 0 else _pick_tile(n, 2304, LANE)
    tk = _pick_tile(kdim, 1024, LANE)
    nk = kdim // tk
    in_specs = [pl.BlockSpec((tm, tk), lambda i, j, k: (i, k)),
                pl.BlockSpec((None, tk, tn), lambda i, j, k: (layer, k, j))]
    args = [x, w]
    if epilogue == "residual":
        in_specs.append(pl.BlockSpec((tm, tn), lambda i, j, k: (i, j)))
        args.append(res)
    return pl.pallas_call(
        functools.partial(_mm_body, nk=nk, epilogue=epilogue),
        grid=(m // tm, n // tn, nk),
        in_specs=in_specs,
        out_specs=pl.BlockSpec((tm, tn), lambda i, j, k: (i, j)),
        out_shape=jax.ShapeDtypeStruct((m, n), out_dtype),
        scratch_shapes=[pltpu.VMEM((tm, tn), F32)],
        compiler_params=pltpu.CompilerParams(
            dimension_semantics=("parallel", "parallel", "arbitrary"), vmem_limit_bytes=VMEM_LIMIT),
        name=name,
    )(*args)


def _merge_body(*refs):
    o_refs, (w_ref, g_ref, out_ref, acc_ref) = refs[:N_BRANCH], refs[N_BRANCH:]
    b = pl.program_id(2)

    for br in range(N_BRANCH):
        @pl.when(b == br)
        def _():
            term = jax.nn.sigmoid(g_ref[...]) * jnp.dot(o_refs[br][...], w_ref[...].astype(BF16),
                                                        preferred_element_type=F32)
            if br == 0:
                acc_ref[...] = term
            elif br < N_BRANCH - 1:
                acc_ref[...] += term
            else:
                out_ref[...] = (acc_ref[...] + term).astype(out_ref.dtype)


def _merge_branches(branches, w_br, layer, h, lay):
    m, mix = branches[0].shape
    d = lay.d_model
    tm = _pick_tile(m, 1024, 8)
    tn = _pick_tile(math.gcd(d, lay.off[22]), 1024, LANE)
    g0, gstep = lay.off[22] // tn, d // tn
    return pl.pallas_call(
        _merge_body,
        grid=(m // tm, d // tn, N_BRANCH),
        in_specs=[pl.BlockSpec((tm, mix), lambda i, j, b: (i, 0)) for _ in range(N_BRANCH)] + [
            pl.BlockSpec((None, None, mix, tn), lambda i, j, b: (layer, b, 0, j)),
            pl.BlockSpec((tm, tn), lambda i, j, b: (i, g0 + b * gstep + j))],
        out_specs=pl.BlockSpec((tm, tn), lambda i, j, b: (i, j)),
        out_shape=jax.ShapeDtypeStruct((m, d), BF16),
        scratch_shapes=[pltpu.VMEM((tm, tn), F32)],
        compiler_params=pltpu.CompilerParams(
            dimension_semantics=("parallel", "parallel", "arbitrary"), vmem_limit_bytes=VMEM_LIMIT),
        name="merge_branches",
    )(*branches, w_br, h)


def _l2norm(x):
    return x * lax.rsqrt(jnp.sum(x * x, axis=-1, keepdims=True) + EPS)


def _rms(x, g):
    y = x * lax.rsqrt(jnp.mean(x * x, axis=-1, keepdims=True) + EPS)
    return y * g


def _chunk_len(t, c):
    return c if t % c == 0 else t


def _rope(x, pos):
    half = x.shape[-1] // 2
    freq = jnp.exp(-math.log(ROPE_BASE) * jnp.arange(half, dtype=F32) / half)
    ang = pos.astype(F32)[:, None] * freq
    cos, sin = jnp.cos(ang)[None, :, None, :], jnp.sin(ang)[None, :, None, :]
    x1, x2 = x[..., :half], x[..., half:]
    return jnp.concatenate([x1 * cos - x2 * sin, x1 * sin + x2 * cos], axis=-1)


def _gated_delta_rule(q, k, v, g, beta, s0):
    b, t, h, _ = q.shape
    dv = v.shape[-1]
    c = _chunk_len(t, GDN_CHUNK)
    n = t // c
    ch = lambda a: jnp.swapaxes(a.reshape(b, n, c, h, *a.shape[3:]), 2, 3)
    q, k, v, g, beta = ch(q), ch(k), ch(v), ch(g), ch(beta)
    gc = jnp.cumsum(g, axis=-1)
    incl = jnp.tril(jnp.ones((c, c), bool))
    strict = jnp.tril(jnp.ones((c, c), F32), -1)
    decay = jnp.exp(jnp.where(incl, gc[..., :, None] - gc[..., None, :], -jnp.inf))
    kb = k * beta[..., None]
    a_mat = jnp.einsum('bnhid,bnhjd->bnhij', kb, k) * decay * strict
    eye = jnp.eye(c, dtype=F32)
    t_mat = lax.linalg.triangular_solve(a_mat + eye, jnp.broadcast_to(eye, a_mat.shape), left_side=True, lower=True)
    u = jnp.einsum('bnhij,bnhjd->bnhid', t_mat, v * beta[..., None])
    w = jnp.einsum('bnhij,bnhjd->bnhid', t_mat, kb * jnp.exp(gc)[..., None])
    qk = jnp.einsum('bnhid,bnhjd->bnhij', q, k) * decay
    qg = q * jnp.exp(gc)[..., None]
    kd = k * jnp.exp(gc[..., -1:] - gc)[..., None]
    glast = jnp.exp(gc[..., -1])

    def step(s, xs):
        u_i, w_i, qk_i, qg_i, kd_i, gl_i = xs
        v_new = u_i - jnp.einsum('bhcd,bhde->bhce', w_i, s)
        o = jnp.einsum('bhcd,bhde->bhce', qg_i, s) + jnp.einsum('bhij,bhje->bhie', qk_i, v_new)
        s = s * gl_i[..., None, None] + jnp.einsum('bhcd,bhce->bhde', kd_i, v_new)
        return s, o

    xs = tuple(jnp.moveaxis(a, 1, 0) for a in (u, w, qk, qg, kd, glast))
    s, o = lax.scan(step, s0, xs)
    return o.transpose(1, 0, 3, 2, 4).reshape(b, t, h, dv), s


def _gdn(qkv, z, b_raw, a_raw, conv_buf, s0, conv_w, a_log, dt_bias, norm_g, nh):
    b, t, _ = qkv.shape
    xc = jnp.concatenate([conv_buf, qkv], axis=1)
    y = jax.nn.silu(sum(xc[:, j:j + t] * conv_w[j] for j in range(GDN_CONV)))
    q, k, v = [a.reshape(b, t, nh, HEAD_DIM) for a in jnp.split(y, 3, axis=-1)]
    q = _l2norm(q) * HEAD_DIM ** -0.5
    k = _l2norm(k)
    beta = jax.nn.sigmoid(b_raw)
    g = -jnp.exp(a_log) * jax.nn.softplus(a_raw + dt_bias)
    o, s = _gated_delta_rule(q, k, v, g, beta, s0)
    o = _rms(o, norm_g) * jax.nn.silu(z.reshape(b, t, nh, HEAD_DIM))
    return o.reshape(b, t, nh * HEAD_DIM), xc[:, t:], s


def _retention(q, k, v, gate, pos, s0, gn_g, nh):
    b, t, _ = q.shape
    h, d = nh, HEAD_DIM
    q = _rope(q.reshape(b, t, h, d), pos)
    k = _rope(k.reshape(b, t, h, d), pos) * d ** -0.5
    v = v.reshape(b, t, h, d)
    lg = jnp.log1p(-jnp.exp2(-5.0 - jnp.arange(h, dtype=F32)))
    c = _chunk_len(t, RET_CHUNK)
    n = t // c
    qc, kc, vc = (a.reshape(b, n, c, h, d) for a in (q, k, v))
    i = jnp.arange(c, dtype=F32)
    diff = i[:, None] - i[None, :]
    dmat = jnp.where(diff >= 0, jnp.exp(jnp.maximum(diff, 0.0)[None] * lg[:, None, None]), 0.0)
    o_in = jnp.einsum('bnhij,bnjhe->bnihe', jnp.einsum('bnihd,bnjhd->bnhij', qc, kc) * dmat, vc)
    q_dec = qc * jnp.exp((i + 1.0)[:, None] * lg)[:, :, None]
    k_dec = kc * jnp.exp((c - 1.0 - i)[:, None] * lg)[:, :, None]
    kv = jnp.einsum('bnjhd,bnjhe->bnhde', k_dec, vc)
    chunk_decay = jnp.exp(c * lg)[:, None, None]

    def step(s, xs):
        qd_i, kv_i = xs
        o = jnp.einsum('bihd,bhde->bihe', qd_i, s)
        return s * chunk_decay + kv_i, o

    s, o_x = lax.scan(step, s0, (jnp.moveaxis(q_dec, 1, 0), jnp.moveaxis(kv, 1, 0)))
    o = (o_in + jnp.moveaxis(o_x, 0, 1)).reshape(b, t, h, d)
    oc = o - jnp.mean(o, axis=-1, keepdims=True)
    o = oc * lax.rsqrt(jnp.mean(oc * oc, axis=-1, keepdims=True) + EPS)
    return jax.nn.silu(gate) * (o.reshape(b, t, h * d) * gn_g), s


_NT = (((1,), (1,)), ((), ()))
GDN_PREP_ROWS = 256
RET_KERNEL_CHUNK = 128


def _iota(shape, dim):
    return lax.broadcasted_iota(jnp.int32, shape, dim)


def _col_block(off, width):
    assert off % width == 0, (off, width)
    return off // width


def _gdn_prep_body(x_ref, halo_ref, cb_ref, w_ref, y_ref, xc_ref, *, tb, nh):
    xc_ref[0:8, :] = jnp.where(pl.program_id(1) == 0, cb_ref[0], halo_ref[0])
    xc_ref[8:, :] = x_ref[0]
    for col in range(3 * nh):
        sl = slice(col * HEAD_DIM, (col + 1) * HEAD_DIM)
        acc = xc_ref[8:, sl] * w_ref[GDN_CONV - 1:GDN_CONV, sl]
        for j in range(GDN_CONV - 1):
            acc = acc + xc_ref[pl.ds(8 - (GDN_CONV - 1) + j, tb), sl] * w_ref[j:j + 1, sl]
        y = acc * jax.nn.sigmoid(acc)
        if col < 2 * nh:
            y = y * lax.rsqrt(jnp.sum(y * y, axis=-1, keepdims=True) + EPS)
            if col < nh:
                y = y * HEAD_DIM ** -0.5
        y_ref[0, :, sl] = y


def _gdn_prep(h3, conv_buf, conv_w, lay):
    b, t, _ = h3.shape
    w = 3 * lay.mix
    tb = _pick_tile(t, GDN_PREP_ROWS, 8)
    cb = jnp.pad(conv_buf, ((0, 0), (8 - conv_buf.shape[1], 0), (0, 0)))
    return pl.pallas_call(
        functools.partial(_gdn_prep_body, tb=tb, nh=lay.nh),
        grid=(b, t // tb),
        in_specs=[pl.BlockSpec((1, tb, w), lambda i, n: (i, n, _col_block(lay.off[0], w))),
                  pl.BlockSpec((1, 8, w), lambda i, n: (i, jnp.maximum(n * (tb // 8) - 1, 0), _col_block(lay.off[0], w))),
                  pl.BlockSpec((1, 8, w), lambda i, n: (i, 0, 0)),
                  pl.BlockSpec((GDN_CONV, w), lambda i, n: (0, 0))],
        out_specs=pl.BlockSpec((1, tb, w), lambda i, n: (i, n, 0)),
        out_shape=jax.ShapeDtypeStruct((b, t, w), F32),
        scratch_shapes=[pltpu.VMEM((tb + 8, w), F32)],
        compiler_params=pltpu.CompilerParams(dimension_semantics=("parallel", "parallel"),
                                             vmem_limit_bytes=VMEM_LIMIT),
        name="gdn_prep",
    )(h3, h3, cb, conv_w)


def _split3(x):
    hi = x.astype(BF16)
    rem = x - hi.astype(F32)
    mid = rem.astype(BF16)
    return hi, mid, (rem - mid.astype(F32)).astype(BF16)


def _mm_hi(x, y):
    xh, xl, _ = _split3(x)
    yh, yl, _ = _split3(y)
    dot = lambda a, b: jnp.dot(a, b, preferred_element_type=F32)
    return dot(xh, yh) + dot(xh, yl) + dot(xl, yh)


def _gdn_body(q_ref, k_ref, v_ref, z_ref, sm_ref, par_ref, ng_ref, s0_ref, o_ref, sout_ref, s_ref, *, c, nh, n_chunks):
    n = pl.program_id(1)

    @pl.when(n == 0)
    def _():
        s_ref[...] = s0_ref[0]

    sm = sm_ref[0]
    x = sm + par_ref[1:2, :]
    softplus = jnp.maximum(x, 0.0) + jnp.log(1.0 + jnp.exp(-jnp.abs(x)))
    gc = -jnp.exp(par_ref[0:1, :]) * softplus
    beta_all = jax.nn.sigmoid(sm)
    row = _iota((c, LANE), 0)
    step = 1
    while step < c:
        gc = gc + jnp.where(row >= step, pltpu.roll(gc, step, 0), 0.0)
        step *= 2
    gct = gc.T
    ii, jj = _iota((c, c), 0), _iota((c, c), 1)
    incl, strict = ii >= jj, ii > jj
    eye = jnp.where(ii == jj, 1.0, 0.0)
    dot = lambda a, b: jnp.dot(a, b, preferred_element_type=F32)
    heads = range(nh)
    sls = [slice(h * HEAD_DIM, (h + 1) * HEAD_DIM) for h in heads]
    gcol = [gc[:, nh + h:nh + h + 1] for h in heads]
    glast = [gc[c - 1:c, nh + h:nh + h + 1] for h in heads]
    beta = [beta_all[:, h:h + 1] for h in heads]
    a, qk, rhs, lhs_q, kd_t = [], [], [], [], []
    for h in heads:
        decay = jnp.where(incl, jnp.exp(jnp.where(incl, gcol[h] - gct[nh + h:nh + h + 1, :], 0.0)), 0.0)
        q, k, v = q_ref[0, :, sls[h]], k_ref[0, :, sls[h]], v_ref[0, :, sls[h]]
        kb = k * beta[h]
        eg = jnp.exp(gcol[h])
        kk = lax.dot_general(jnp.concatenate([kb, q], axis=0).astype(BF16), k.astype(BF16), _NT,
                             preferred_element_type=F32)
        a.append(jnp.where(strict, kk[:c] * decay, 0.0))
        qk.append((kk[c:] * decay).astype(BF16))
        rhs.append(jnp.concatenate([v * beta[h], kb * eg], axis=1).astype(BF16))
        lhs_q.append((q * eg).astype(BF16))
        kd_t.append((k * jnp.exp(glast[h] - gcol[h])).T.astype(BF16))
    t_inv, p = [eye - a[h] for h in heads], a
    power = 2
    while power < c:
        p = [_mm_hi(p[h], p[h]) for h in heads]
        t_inv = [t_inv[h] + _mm_hi(t_inv[h], p[h]) for h in heads]
        power *= 2
    uw = [dot(t_inv[h].astype(BF16), rhs[h]) for h in heads]
    ws = [dot(jnp.concatenate([uw[h][:, HEAD_DIM:].astype(BF16), lhs_q[h]], axis=0), s_ref[h].astype(BF16))
          for h in heads]
    v_new = [(uw[h][:, :HEAD_DIM] - ws[h][:c]).astype(BF16) for h in heads]
    for h in heads:
        s_ref[h] = s_ref[h] * jnp.exp(glast[h]) + dot(kd_t[h], v_new[h])
    for h in heads:
        o = ws[h][c:] + dot(qk[h], v_new[h])
        o = o * lax.rsqrt(jnp.mean(o * o, axis=-1, keepdims=True) + EPS) * ng_ref[...]
        z = z_ref[0, :, sls[h]]
        o_ref[0, :, sls[h]] = (o * (z * jax.nn.sigmoid(z))).astype(o_ref.dtype)

    @pl.when(n == n_chunks - 1)
    def _():
        sout_ref[0] = s_ref[...]


def _gdn_chunks(y, h3, s0, a_log, dt_bias, norm_g, lay):
    b, t, _ = y.shape
    nh, mix = lay.nh, lay.mix
    c = GDN_CHUNK
    n_chunks = t // c
    par = jnp.zeros((2, LANE), F32).at[0, nh:2 * nh].set(a_log).at[1, nh:2 * nh].set(dt_bias)
    assert lay.off[3] - lay.off[2] == nh
    tok = lambda off, width: pl.BlockSpec((1, c, width), lambda i, n: (i, n, _col_block(off, width)))
    state = pl.BlockSpec((1, nh, HEAD_DIM, HEAD_DIM), lambda i, n: (i, 0, 0, 0))
    return pl.pallas_call(
        functools.partial(_gdn_body, c=c, nh=nh, n_chunks=n_chunks),
        grid=(b, n_chunks),
        in_specs=[tok(0, mix), tok(mix, mix), tok(2 * mix, mix), tok(lay.off[1], mix), tok(lay.off[2], LANE),
                  pl.BlockSpec((2, LANE), lambda i, n: (0, 0)), pl.BlockSpec((1, HEAD_DIM), lambda i, n: (0, 0)),
                  state],
        out_specs=[pl.BlockSpec((1, c, mix), lambda i, n: (i, n, 0)), state],
        out_shape=[jax.ShapeDtypeStruct((b, t, mix), BF16), jax.ShapeDtypeStruct(s0.shape, F32)],
        scratch_shapes=[pltpu.VMEM((nh, HEAD_DIM, HEAD_DIM), F32)],
        compiler_params=pltpu.CompilerParams(dimension_semantics=("parallel", "arbitrary"),
                                             vmem_limit_bytes=VMEM_LIMIT),
        name="gdn_chunks",
    )(y, y, y, h3, h3, par, norm_g.reshape(1, HEAD_DIM), s0)


def _ret_body(q_ref, k_ref, v_ref, g_ref, cos_ref, sin_ref, gn_ref, s0_ref, o_ref, sout_ref, s_ref, *, c, nh, n_chunks):
    n = pl.program_id(1)

    @pl.when(n == 0)
    def _():
        s_ref[...] = s0_ref[0]

    cos, sin = cos_ref[...], sin_ref[...]
    ii, jj = _iota((c, c), 0), _iota((c, c), 1)
    lower = ii >= jj
    dist = jnp.where(lower, ii - jj, 0).astype(F32)
    icol = _iota((c, 1), 0).astype(F32)
    dot = lambda a, b: jnp.dot(a, b, preferred_element_type=F32)
    rope = lambda x: x * cos + pltpu.roll(x, HEAD_DIM // 2, 1) * sin
    for h in range(nh):
        lg = math.log1p(-2.0 ** (-5.0 - h))
        sl = slice(h * HEAD_DIM, (h + 1) * HEAD_DIM)
        q = rope(q_ref[0, :, sl])
        k = rope(k_ref[0, :, sl]) * HEAD_DIM ** -0.5
        v_bf = v_ref[0, :, sl].astype(BF16)
        qk = lax.dot_general(q.astype(BF16), k.astype(BF16), _NT, preferred_element_type=F32)
        qk = qk * jnp.where(lower, jnp.exp(dist * lg), 0.0)
        s_bf = s_ref[h].astype(BF16)
        o = dot(qk.astype(BF16), v_bf) + dot((q * jnp.exp((icol + 1.0) * lg)).astype(BF16), s_bf)
        k_dec = k * jnp.exp((c - 1.0 - icol) * lg)
        s_ref[h] = s_ref[h] * math.exp(c * lg) + dot(k_dec.T.astype(BF16), v_bf)
        oc = o - jnp.mean(o, axis=-1, keepdims=True)
        o = oc * lax.rsqrt(jnp.mean(oc * oc, axis=-1, keepdims=True) + EPS)
        gate = g_ref[0, :, sl]
        o_ref[0, :, sl] = (gate * jax.nn.sigmoid(gate) * (o * gn_ref[:, sl])).astype(o_ref.dtype)

    @pl.when(n == n_chunks - 1)
    def _():
        sout_ref[0] = s_ref[...]


def _ret_chunks(h3, pos, s0, gn_g, lay):
    b, t, _ = h3.shape
    nh, mix = lay.nh, lay.mix
    c = _pick_tile(t, RET_KERNEL_CHUNK, 8)
    n_chunks = t // c
    half = HEAD_DIM // 2
    freq = jnp.exp(-math.log(ROPE_BASE) * jnp.arange(half, dtype=F32) / half)
    ang = pos.astype(F32)[:, None] * freq
    cos, sin = jnp.cos(ang), jnp.sin(ang)
    cos2, sin2 = jnp.concatenate([cos, cos], axis=-1), jnp.concatenate([-sin, sin], axis=-1)
    tok = lambda off: pl.BlockSpec((1, c, mix), lambda i, n: (i, n, _col_block(off, mix)))
    table = pl.BlockSpec((c, HEAD_DIM), lambda i, n: (n, 0))
    state = pl.BlockSpec((1, nh, HEAD_DIM, HEAD_DIM), lambda i, n: (i, 0, 0, 0))
    return pl.pallas_call(
        functools.partial(_ret_body, c=c, nh=nh, n_chunks=n_chunks),
        grid=(b, n_chunks),
        in_specs=[tok(lay.off[4]), tok(lay.off[5]), tok(lay.off[6]), tok(lay.off[7]), table, table,
                  pl.BlockSpec((1, mix), lambda i, n: (0, 0)), state],
        out_specs=[pl.BlockSpec((1, c, mix), lambda i, n: (i, n, 0)), state],
        out_shape=[jax.ShapeDtypeStruct((b, t, mix), BF16), jax.ShapeDtypeStruct(s0.shape, F32)],
        scratch_shapes=[pltpu.VMEM((nh, HEAD_DIM, HEAD_DIM), F32)],
        compiler_params=pltpu.CompilerParams(dimension_semantics=("parallel", "arbitrary"),
                                             vmem_limit_bytes=VMEM_LIMIT),
        name="retention_chunks",
    )(h3, h3, h3, h3, cos2, sin2, gn_g.reshape(1, mix), s0)


INT_MIN = -2 ** 31
PAD_SCORE = -3e38


def _sortable(x):
    b = pltpu.bitcast(x, jnp.int32)
    return b ^ ((b >> 31) & 0x7FFFFFFF)


def _kth_largest(key_ref, nc, k, n_live, two_bits):
    def count_ge(t):
        if n_live is None:
            acc = jnp.where(key_ref[0] >= t, 1.0, 0.0)
            for c in range(1, nc):
                acc = acc + jnp.where(key_ref[c] >= t, 1.0, 0.0)
        else:
            acc = lax.fori_loop(0, n_live, lambda c, acc: acc + jnp.where(key_ref[c] >= t, 1.0, 0.0),
                                jnp.zeros(key_ref.shape[1:], F32))
        return jnp.sum(acc, axis=-1, keepdims=True)

    tq = key_ref.shape[1]
    zero = jnp.zeros((tq, 1), jnp.int32)
    base = jnp.where(count_ge(zero) >= k, zero, jnp.full((tq, 1), INT_MIN, jnp.int32))

    def one_bit(bit, base):
        cand = base | (jnp.int32(1) << bit)
        return jnp.where(count_ge(cand) >= k, cand, base)

    if not two_bits:
        return lax.fori_loop(0, 31, lambda i, base: one_bit(30 - i, base), base)

    def two_bit_step(i, base):
        lo = 29 - 2 * i
        c1, c2, c3 = (base | (jnp.int32(v) << lo) for v in (1, 2, 3))
        n1, n2, n3 = count_ge(c1), count_ge(c2), count_ge(c3)
        return jnp.where(n3 >= k, c3, jnp.where(n2 >= k, c2, jnp.where(n1 >= k, c1, base)))

    return one_bit(0, lax.fori_loop(0, 15, two_bit_step, base))


def _topk_blocks(key_ref, nc, k, n_live=None, two_bits=False):
    _, tq, lc = key_ref.shape
    thr = _kth_largest(key_ref, nc, k, n_live, two_bits)
    n_gt = jnp.where(key_ref[0] > thr, 1.0, 0.0)
    for c in range(1, nc):
        n_gt = n_gt + jnp.where(key_ref[c] > thr, 1.0, 0.0)
    need = k - jnp.sum(n_gt, axis=-1, keepdims=True)
    tri = jnp.where(_iota((LANE, LANE), 0) <= _iota((LANE, LANE), 1), 1.0, 0.0).astype(BF16)
    carry = jnp.zeros((tq, 1), F32)
    for c in range(nc):
        for j in range(lc // LANE):
            kk = key_ref[c, :, j * LANE:(j + 1) * LANE]
            eq = kk == thr
            eqf = jnp.where(eq, 1.0, 0.0)
            prefix = jnp.dot(eqf.astype(BF16), tri, preferred_element_type=F32) + carry
            carry = carry + jnp.sum(eqf, axis=-1, keepdims=True)
            yield c, j, (kk > thr) | (eq & (prefix <= need))


def _dsa_select_body(qi_ref, sm_ref, ki_ref, o_ref, qh32_ref, qh_ref, d_ref, key_ref, *, tq, lc, nc, pos0, topk, wi_off):
    q_first = pos0 + pl.program_id(1) * tq
    qpos = q_first + _iota((tq, 1), 0)
    n_live = jnp.minimum(nc, (q_first + tq - 1) // lc + 1)
    qi = qi_ref[0]
    for h in range(IDX_HEADS):
        qh32_ref[h * tq:(h + 1) * tq, :] = qi[:, h * IDX_DIM:(h + 1) * IDX_DIM]
    qh_ref[...] = qh32_ref[...].astype(BF16)
    sm = sm_ref[0]
    for c in range(nc):
        @pl.when(c < n_live)
        def _():
            kic = ki_ref[0, c * lc:(c + 1) * lc, :IDX_DIM].astype(BF16)
            d_ref[...] = lax.dot_general(qh_ref[...], kic, _NT, preferred_element_type=F32)
            score = jnp.zeros((tq, lc), F32)
            for h in range(IDX_HEADS):
                score = score + sm[:, wi_off + h:wi_off + h + 1] * jnp.maximum(d_ref[h * tq:(h + 1) * tq, :], 0.0)
            kpos = c * lc + _iota((tq, lc), 1)
            key_ref[c] = _sortable(jnp.where(kpos <= qpos, score, NEG))

        @pl.when(c >= n_live)
        def _():
            key_ref[c] = _sortable(jnp.full((tq, lc), NEG, F32))
    pieces = []
    for c, j, sel in _topk_blocks(key_ref, nc, topk, n_live):
        kpos = c * lc + j * LANE + _iota((tq, LANE), 1)
        pieces.append(jnp.where(sel & (kpos <= qpos), 1.0, 0.0))
        if len(pieces) == lc // LANE:
            o_ref[0, c] = jnp.concatenate(pieces, axis=1).astype(BF16)
            pieces = []


def _dsa_select(hq, qi_off, sm_off, rows, ki_off, *, tq, lc, pos0, l_real, wi_off):
    b, t, _ = hq.shape
    l = rows.shape[1]
    nc = l // lc
    topk = min(DSA_TOPK_MAX, l_real // 4)
    qw = IDX_HEADS * IDX_DIM
    return pl.pallas_call(
        functools.partial(_dsa_select_body, tq=tq, lc=lc, nc=nc, pos0=pos0, topk=topk, wi_off=wi_off),
        grid=(b, t // tq),
        in_specs=[pl.BlockSpec((1, tq, qw), lambda i, q: (i, q, _col_block(qi_off, qw))),
                  pl.BlockSpec((1, tq, LANE), lambda i, q: (i, q, _col_block(sm_off, LANE))),
                  pl.BlockSpec((1, l, LANE), lambda i, q: (i, 0, _col_block(ki_off, LANE)))],
        out_specs=pl.BlockSpec((1, nc, tq, lc), lambda i, q: (i, 0, q, 0)),
        out_shape=jax.ShapeDtypeStruct((b, nc, t, lc), BF16),
        scratch_shapes=[pltpu.VMEM((IDX_HEADS * tq, IDX_DIM), F32), pltpu.VMEM((IDX_HEADS * tq, IDX_DIM), BF16),
                        pltpu.VMEM((IDX_HEADS * tq, lc), F32), pltpu.VMEM((nc, tq, lc), jnp.int32)],
        compiler_params=pltpu.CompilerParams(dimension_semantics=("parallel", "arbitrary"),
                                             vmem_limit_bytes=VMEM_LIMIT),
        name="dsa_select",
    )(hq, hq, rows)


def _nsa_select_body(q_ref, kc0_ref, kc1_ref, vc0_ref, vc1_ref, wk_ref, wv_ref, bk_ref, bv_ref, ov_ref, e_ref,
                     ocmp_ref, mask_ref, ck_ref, cv_ref, cacc_ref, key_ref, *, tq, lc, nc, pos0, l16, ns, n_sel, grp):
    qb = pl.program_id(1)

    @pl.when(qb == 0)
    def _():
        for srcs, w_ref, b_ref, dst in (((kc0_ref, kc1_ref), wk_ref, bk_ref, ck_ref),
                                        ((vc0_ref, vc1_ref), wv_ref, bv_ref, cv_ref)):
            for g in range(KV_HEADS):
                for c in range(CMP_STRIDE):
                    x = srcs[g][0, pl.ds(c, l16, stride=CMP_STRIDE), :].astype(BF16)
                    part = jnp.dot(x, w_ref[c], preferred_element_type=F32)
                    if c == 0:
                        cacc_ref[...] = part
                    else:
                        cacc_ref[...] += part
                acc = cacc_ref[...]
                summ = acc[:, :HEAD_DIM] + pltpu.roll(acc[:, HEAD_DIM:], l16 - 1, 0) + b_ref[...]
                dst[g] = summ.astype(BF16)

    qpos = pos0 + qb * tq + _iota((tq, 1), 0)
    vis = (_iota((tq, l16), 1) * CMP_STRIDE + (CMP_BLOCK - 1)) <= qpos
    jcol = _iota((tq, ns), 1)
    cur = qpos >> int(math.log2(SEL_BLOCK))
    forced = (jcol == 0) | (jcol == cur) | (jcol == cur - 1)
    admissible = jcol * SEL_BLOCK <= qpos
    scale = HEAD_DIM ** -0.5
    for g in range(KV_HEADS):
        imp_c = jnp.zeros((tq, l16), F32)
        for r in range(grp):
            hd = g * grp + r
            q = q_ref[0][:, hd * HEAD_DIM:(hd + 1) * HEAD_DIM].astype(BF16)
            s = lax.dot_general(q, ck_ref[g], _NT, preferred_element_type=F32) * scale
            s = jnp.where(vis, s, NEG)
            e = jnp.exp(s - jnp.max(s, axis=-1, keepdims=True))
            p = jnp.where(vis, e / jnp.sum(e, axis=-1, keepdims=True), 0.0)
            imp_c = imp_c + p
            ocmp_ref[0, :, hd * HEAD_DIM:(hd + 1) * HEAD_DIM] = jnp.dot(
                p.astype(BF16), cv_ref[g], preferred_element_type=F32)
        hi = imp_c.astype(BF16)
        rem = imp_c - hi.astype(F32)
        mid = rem.astype(BF16)
        lo = (rem - mid.astype(F32)).astype(BF16)
        ov = ov_ref[...]
        imp = (jnp.dot(hi, ov, preferred_element_type=F32) + jnp.dot(mid, ov, preferred_element_type=F32)
               + jnp.dot(lo, ov, preferred_element_type=F32))
        imp = jnp.where(admissible, jnp.where(forced, imp + FORCE_BONUS, imp), NEG)
        imp = jnp.where(jcol < n_sel, imp, PAD_SCORE)
        key_ref[0, g * tq:(g + 1) * tq, :] = _sortable(imp)
    sel_all = jnp.concatenate(
        [jnp.where(s_, 1.0, 0.0) for _, _, s_ in _topk_blocks(key_ref, 1, min(SEL_TOPN, n_sel), two_bits=True)],
        axis=1)
    for g in range(KV_HEADS):
        sel = sel_all[g * tq:(g + 1) * tq].astype(BF16)
        for c in range(nc):
            tok = jnp.dot(sel, e_ref[:, c * lc:(c + 1) * lc], preferred_element_type=F32)
            kpos = c * lc + _iota((tq, lc), 1)
            mask_ref[g, c] = jnp.where((tok > 0.5) & (kpos <= qpos), 1.0, 0.0).astype(BF16)


def _nsa_select(hq, q_off, rows, kc_off, vc_off, cwk, cwv, pek, pev, *, tq, lc, pos0, l_real, nh):
    b, t, _ = hq.shape
    l = rows.shape[1]
    nc = l // lc
    grp = nh // KV_HEADS
    n_cmp = (l_real - CMP_BLOCK) // CMP_STRIDE + 1
    l16 = n_cmp + 1
    n_sel = -(-l_real // SEL_BLOCK)
    ns = _round_up(n_sel, LANE)
    pack = lambda w: jnp.concatenate([w[:CMP_STRIDE], w[CMP_STRIDE:]], axis=-1).astype(BF16)
    bias = lambda pe, w: jnp.einsum('cd,cde->e', pe, w, precision=lax.Precision.HIGHEST).reshape(1, HEAD_DIM)
    c0 = np.arange(l16)[:, None] * CMP_STRIDE
    s0 = np.arange(ns)[None, :] * SEL_BLOCK
    overlap = jnp.asarray((c0 < s0 + SEL_BLOCK) & (c0 + CMP_BLOCK > s0), BF16)
    expand = jnp.asarray(np.arange(l)[None, :] // SEL_BLOCK == np.arange(ns)[:, None], BF16)
    qw = nh * HEAD_DIM
    lrows = CMP_STRIDE * l16
    const = lambda shape: pl.BlockSpec(shape, lambda i, q: (0,) * len(shape))
    head_rows = lambda off, g: pl.BlockSpec((1, lrows, HEAD_DIM), lambda i, q: (i, 0, _col_block(off, HEAD_DIM) + g))
    assert KV_HEADS == 2
    return pl.pallas_call(
        functools.partial(_nsa_select_body, tq=tq, lc=lc, nc=nc, pos0=pos0, l16=l16, ns=ns, n_sel=n_sel, grp=grp),
        grid=(b, t // tq),
        in_specs=[pl.BlockSpec((1, tq, qw), lambda i, q: (i, q, _col_block(q_off, qw))),
                  head_rows(kc_off, 0), head_rows(kc_off, 1), head_rows(vc_off, 0), head_rows(vc_off, 1),
                  const((CMP_STRIDE, HEAD_DIM, 2 * HEAD_DIM)), const((CMP_STRIDE, HEAD_DIM, 2 * HEAD_DIM)),
                  const((1, HEAD_DIM)), const((1, HEAD_DIM)), const((l16, ns)), const((ns, l))],
        out_specs=[pl.BlockSpec((1, tq, qw), lambda i, q: (i, q, 0)),
                   pl.BlockSpec((KV_HEADS, nc, tq, lc), lambda i, q: (i, 0, q, 0))],
        out_shape=[jax.ShapeDtypeStruct((b, t, qw), F32),
                   jax.ShapeDtypeStruct((b * KV_HEADS, nc, t, lc), BF16)],
        scratch_shapes=[pltpu.VMEM((KV_HEADS, l16, HEAD_DIM), BF16), pltpu.VMEM((KV_HEADS, l16, HEAD_DIM), BF16),
                        pltpu.VMEM((l16, 2 * HEAD_DIM), F32), pltpu.VMEM((1, KV_HEADS * tq, ns), jnp.int32)],
        compiler_params=pltpu.CompilerParams(dimension_semantics=("parallel", "arbitrary"),
                                             vmem_limit_bytes=VMEM_LIMIT),
        name="nsa_select",
    )(hq, rows, rows, rows, rows, pack(cwk), pack(cwv), bias(pek, cwk), bias(pev, cwv), overlap, expand)


def _attend_body(*refs, banded, tq, lc, nc, pos0, kpos0, grp):
    if banded:
        q_ref, k_ref, v_ref, o_ref, s_ref, acc_ref = refs
    else:
        q_ref, k_ref, v_ref, m_ref, o_ref, s_ref, acc_ref = refs
    q_first = pos0 + pl.program_id(2) * tq
    qpos = q_first + _iota((tq, 1), 0)
    c_hi = jnp.minimum(nc, (q_first + tq - 1 - kpos0) // lc + 1)
    c_lo = jnp.maximum(q_first - WINDOW - kpos0, 0) // lc if banded else 0
    scale = HEAD_DIM ** -0.5
    qs = [q_ref[0][:, r * HEAD_DIM:(r + 1) * HEAD_DIM].astype(BF16) for r in range(grp)]

    unroll = 2 if nc % 2 == 0 else 1
    g_lo, g_hi = c_lo // unroll, (c_hi + unroll - 1) // unroll

    def scores(g, ms):
        ms = list(ms)
        for u in range(unroll):
            c = g * unroll + u
            kc = k_ref[0, pl.ds(pl.multiple_of(c * lc, lc), lc), :].astype(BF16)
            if banded:
                kpos = kpos0 + c * lc + _iota((tq, lc), 1)
                valid = (kpos <= qpos) & (qpos - kpos <= WINDOW)
            else:
                valid = m_ref[0, c].astype(F32) > 0.5
            for r in range(grp):
                s = jnp.where(valid, lax.dot_general(qs[r], kc, _NT, preferred_element_type=F32) * scale, NEG)
                s_ref[r, c] = s
                ms[r] = jnp.maximum(ms[r], jnp.max(s, axis=-1, keepdims=True))
        return tuple(ms)

    ms = lax.fori_loop(g_lo, g_hi, scores, tuple(jnp.full((tq, 1), NEG, F32) for _ in range(grp)))
    acc_ref[...] = jnp.zeros_like(acc_ref)

    def values(g, ls):
        ls = list(ls)
        cs = [g * unroll + u for u in range(unroll)]
        vcs = [v_ref[0, pl.ds(pl.multiple_of(c * lc, lc), lc), :].astype(BF16) for c in cs]
        for r in range(grp):
            pv = None
            for c, vc in zip(cs, vcs):
                p = jnp.exp(s_ref[r, c] - ms[r])
                d = jnp.dot(p.astype(BF16), vc, preferred_element_type=F32)
                pv = d if pv is None else pv + d
                ls[r] = ls[r] + jnp.sum(p, axis=-1, keepdims=True)
            acc_ref[r] += pv
        return tuple(ls)

    ls = lax.fori_loop(g_lo, g_hi, values, tuple(jnp.zeros((tq, 1), F32) for _ in range(grp)))
    for r in range(grp):
        o_ref[0, :, r * HEAD_DIM:(r + 1) * HEAD_DIM] = (acc_ref[r] / ls[r]).astype(o_ref.dtype)


def _attend(hq, q_off, kv, k_off, v_off, mask, *, tq, lc, pos0, kpos0, nh, mask_per_group=False, out_dtype=F32,
            name="attend"):
    b, t, _ = hq.shape
    l = kv.shape[1]
    nc = l // lc
    grp = nh // KV_HEADS
    qw = grp * HEAD_DIM
    banded = mask is None
    in_specs = [pl.BlockSpec((1, tq, qw), lambda i, g, q: (i, q, _col_block(q_off, qw) + g)),
                pl.BlockSpec((1, l, HEAD_DIM), lambda i, g, q: (i, 0, _col_block(k_off, HEAD_DIM) + g)),
                pl.BlockSpec((1, l, HEAD_DIM), lambda i, g, q: (i, 0, _col_block(v_off, HEAD_DIM) + g))]
    args = [hq, kv, kv]
    if not banded:
        if mask_per_group:
            in_specs.append(pl.BlockSpec((1, nc, tq, lc), lambda i, g, q: (i * KV_HEADS + g, 0, q, 0)))
        else:
            in_specs.append(pl.BlockSpec((1, nc, tq, lc), lambda i, g, q: (i, 0, q, 0)))
        args.append(mask)
    return pl.pallas_call(
        functools.partial(_attend_body, banded=banded, tq=tq, lc=lc, nc=nc, pos0=pos0, kpos0=kpos0, grp=grp),
        grid=(b, KV_HEADS, t // tq),
        in_specs=in_specs,
        out_specs=pl.BlockSpec((1, tq, qw), lambda i, g, q: (i, q, g)),
        out_shape=jax.ShapeDtypeStruct((b, t, nh * HEAD_DIM), out_dtype),
        scratch_shapes=[pltpu.VMEM((grp, nc, tq, lc), F32), pltpu.VMEM((grp, tq, HEAD_DIM), F32)],
        compiler_params=pltpu.CompilerParams(dimension_semantics=("parallel", "parallel", "arbitrary"),
                                             vmem_limit_bytes=VMEM_LIMIT),
        name=name,
    )(*args)


def _nsa_combine_body(a_ref, b_ref, c_ref, sm_ref, o_ref, *, gate_off, nh):
    gate = jax.nn.sigmoid(sm_ref[...])
    for hd in range(nh):
        sl = slice(hd * HEAD_DIM, (hd + 1) * HEAD_DIM)
        col = lambda br: gate[:, gate_off + br * nh + hd:gate_off + br * nh + hd + 1]
        o = col(0) * a_ref[:, sl] + col(1) * b_ref[:, sl] + col(2) * c_ref[:, sl]
        o_ref[:, sl] = o.astype(o_ref.dtype)


def _nsa_combine(o_cmp, o_sel, o_win, h, sm_off, gate_off, nh):
    m, w = o_cmp.shape
    tm = _pick_tile(m, 512, 8)
    blk = pl.BlockSpec((tm, w), lambda i: (i, 0))
    return pl.pallas_call(
        functools.partial(_nsa_combine_body, gate_off=gate_off, nh=nh),
        grid=(m // tm,),
        in_specs=[blk, blk, blk, pl.BlockSpec((tm, LANE), lambda i: (i, _col_block(sm_off, LANE)))],
        out_specs=blk,
        out_shape=jax.ShapeDtypeStruct((m, w), BF16),
        compiler_params=pltpu.CompilerParams(dimension_semantics=("parallel",), vmem_limit_bytes=VMEM_LIMIT),
        name="nsa_combine",
    )(o_cmp, o_sel, o_win, h)


def _sparse_mixers(h, hq, kv, kv_off, win, win_off, win_pos0, lay, cw, *, tq, pos0, l_real, t_real):
    cwk, cwv, pek, pev = cw
    nh = lay.nh
    b = hq.shape[0]
    lc = _pick_tile(kv.shape[1], 32 * 1024 // tq, LANE)
    lcw = _pick_tile(win.shape[1], 16 * 1024 // tq, LANE)
    sm_off = lay.off[2]
    offs = np.concatenate([[0], np.cumsum(PAGED_SIZES)]).tolist()
    common = dict(tq=tq, pos0=pos0, nh=nh)
    o_cmp, sel_mask = _nsa_select(hq, lay.off[8], kv, kv_off + offs[0], kv_off + offs[1], cwk, cwv, pek, pev,
                                  lc=lc, l_real=l_real, **common)
    o_sel = _attend(hq, lay.off[8], kv, kv_off + offs[2], kv_off + offs[3], sel_mask, lc=lc, kpos0=0,
                    mask_per_group=True, name="nsa_selected", **common)
    o_win = _attend(hq, lay.off[8], win, win_off, win_off + KVW, None, lc=lcw, kpos0=win_pos0,
                    name="nsa_window", **common)
    dsa_mask = _dsa_select(hq, lay.off[19], sm_off, kv, kv_off + offs[6], tq=tq, lc=lc, pos0=pos0, l_real=l_real,
                           wi_off=lay.off[21] - sm_off)
    o_d = _attend(hq, lay.off[16], kv, kv_off + offs[4], kv_off + offs[5], dsa_mask, lc=lc, kpos0=0,
                  out_dtype=BF16, name="dsa_attend", **common)
    unpad = lambda o: o[:, :t_real].reshape(b * t_real, nh * HEAD_DIM)
    o_c = _nsa_combine(unpad(o_cmp), unpad(o_sel), unpad(o_win), h, sm_off, lay.off[15] - sm_off, nh)
    return o_c, unpad(o_d)


def _gather_body(pt_ref, x_ref, *o_refs, n_pages):
    del pt_ref
    live = pl.program_id(1) < n_pages

    @pl.when(live)
    def _():
        for l, o_ref in enumerate(o_refs):
            for c0 in range(0, PAGED_W, LANE):
                cw = min(LANE, PAGED_W - c0)
                o_ref[0, :, c0:c0 + cw] = x_ref[0, l, c0:c0 + cw, :].T
            o_ref[0, :, PAGED_W:] = jnp.zeros((o_ref.shape[1], o_ref.shape[2] - PAGED_W), F32)

    @pl.when(jnp.logical_not(live))
    def _():
        for o_ref in o_refs:
            o_ref[...] = jnp.zeros(o_ref.shape, F32)


def _paged_gather(cache_kv, page_table, l_pad):
    _, page, depth, width = cache_kv.shape
    b, n_pages = page_table.shape
    wp = _round_up(width, LANE)
    cache_t = jnp.transpose(cache_kv, (0, 2, 3, 1))
    out = pl.BlockSpec((1, page, wp), lambda i, p, pt: (i, p, 0))
    return pl.pallas_call(
        functools.partial(_gather_body, n_pages=n_pages),
        grid_spec=pltpu.PrefetchScalarGridSpec(
            num_scalar_prefetch=1,
            grid=(b, l_pad // page),
            in_specs=[pl.BlockSpec((1, depth, width, page),
                                   lambda i, p, pt: (pt[i * n_pages + jnp.minimum(p, n_pages - 1)], 0, 0, 0))],
            out_specs=[out] * depth),
        out_shape=[jax.ShapeDtypeStruct((b, l_pad, wp), F32)] * depth,
        compiler_params=pltpu.CompilerParams(dimension_semantics=("parallel", "arbitrary"),
                                             vmem_limit_bytes=VMEM_LIMIT),
        name="paged_gather",
    )(page_table.reshape(-1), cache_t)


def _layer(x, q_pos, lw, layer, past, lay):
    (g_mix, g_mlp, w_in_p, conv_w, a_log, dt_bias, gdn_g, ret_g, cwk, cwv, pek, pev, w_br, w_o, w_up_l, w_down_l) = lw
    conv_buf, s_gdn, s_ret, win_buf, past_kv, past_len = past
    b, t, d = x.shape
    m = b * t
    mix, nh = lay.mix, lay.nh
    x2d = x.reshape(m, d)
    h = _matmul(_rmsnorm(x2d, g_mix, BF16), w_in_p, layer, name="in_proj")
    h3 = h.reshape(b, t, lay.width)
    seg = lambda idx: h3[:, :, lay.off[idx]:lay.off[idx] + lay.in_sizes[idx]]

    if t % GDN_CHUNK == 0:
        o_a, s_gdn_new = _gdn_chunks(_gdn_prep(h3, conv_buf, conv_w, lay), h3, s_gdn, a_log, dt_bias, gdn_g, lay)
        conv_new = jnp.concatenate([conv_buf, seg(0)[:, t - (GDN_CONV - 1):]], axis=1)[:, -(GDN_CONV - 1):]
        o_b, s_ret_new = _ret_chunks(h3, q_pos, s_ret, ret_g, lay)
    else:
        o_a, conv_new, s_gdn_new = _gdn(seg(0), seg(1), seg(2), seg(3), conv_buf, s_gdn, conv_w, a_log, dt_bias,
                                        gdn_g, nh)
        o_b, s_ret_new = _retention(seg(4), seg(5), seg(6), seg(7), q_pos, s_ret, ret_g, nh)

    new_rows = h3[:, :, lay.off[9]:lay.off[9] + PAGED_W]
    win_rows = h3[:, :, lay.off[13]:lay.off[13] + 2 * KVW]
    cw = (cwk, cwv, pek, pev)
    if past_kv is None:
        win_new = win_rows[:, t - min(WINDOW, t):]
        o_c, o_d = _sparse_mixers(h, h3, h3, lay.off[9], h3, lay.off[13], 0, lay, cw,
                                  tq=Q_BLOCK, pos0=0, l_real=t, t_real=t)
    else:
        wb = win_buf.shape[1]
        tq = _round_up(t, 8)
        pad_rows = lambda a, n: jnp.pad(a, ((0, 0), (0, n - a.shape[1]), (0, 0)))
        lane_pad = past_kv.shape[2] - PAGED_W
        kv = lax.dynamic_update_slice(past_kv, jnp.pad(new_rows, ((0, 0), (0, 0), (0, lane_pad))), (0, past_len, 0))
        win_all = jnp.concatenate([win_buf, win_rows], axis=1)
        win_new = win_all[:, t:]
        o_c, o_d = _sparse_mixers(h, pad_rows(h3, tq), kv, 0, pad_rows(win_all, _round_up(wb + t, LANE)), 0,
                                  past_len - wb, lay, cw, tq=tq, pos0=past_len, l_real=past_len + t, t_real=t)

    branches = [o.reshape(m, mix).astype(BF16) for o in (o_a, o_b, o_c, o_d)]
    mixed = _merge_branches(branches, w_br, layer, h, lay)
    x2 = _matmul(mixed, w_o, layer, epilogue="residual", res=x2d, name="out_proj")
    up = _matmul(_rmsnorm(x2, g_mlp, BF16), w_up_l, layer, epilogue="relu2", out_dtype=BF16, name="ffn_up")
    x3 = _matmul(up, w_down_l, layer, epilogue="residual", res=x2, name="ffn_down")
    return x3.reshape(b, t, d), (new_rows, conv_new, s_gdn_new, s_ret_new, win_new)


def kernel(x_prompt, x_sample, cache_kv, cache_nsa_window, state_gdn, state_gdn_conv, state_retention, page_table, norm_mix, norm_mlp, norm_final, w_in, gdn_conv_w, gdn_a_log, gdn_dt_bias, gdn_norm, ret_norm, nsa_cmp_wk, nsa_cmp_wv, nsa_cmp_pe_k, nsa_cmp_pe_v, w_branch, w_out, w_up, w_down):
    bp, tp, d = x_prompt.shape
    bs, ts, _ = x_sample.shape
    depth = w_in.shape[0]
    lay = _Layout(d)
    past_len = page_table.shape[1] * cache_kv.shape[1]
    pos_p = jnp.arange(tp, dtype=jnp.int32)
    pos_s = past_len + jnp.arange(ts, dtype=jnp.int32)
    zero_conv = jnp.zeros((bp, GDN_CONV - 1, 3 * lay.mix), F32)
    zero_state = jnp.zeros((bp, lay.nh, HEAD_DIM, HEAD_DIM), F32)
    w_in_p = _pack_w_in(w_in, lay)
    w_br_b, w_up_b = w_branch, w_up
    w_o_b, w_down_b = w_out.astype(BF16), w_down.astype(BF16)
    past_kv = _paged_gather(cache_kv, page_table, _round_up(past_len + ts, 2 * LANE))
    hp, hs = x_prompt, x_sample
    out_p, out_s = [], []
    for l in range(depth):
        lw = (norm_mix[l], norm_mlp[l], w_in_p, gdn_conv_w[l], gdn_a_log[l], gdn_dt_bias[l], gdn_norm[l],
              ret_norm[l], nsa_cmp_wk[l], nsa_cmp_wv[l], nsa_cmp_pe_k[l], nsa_cmp_pe_v[l], w_br_b,
              w_o_b, w_up_b, w_down_b)
        hp, st_p = _layer(hp, pos_p, lw, l, (zero_conv, zero_state, zero_state, None, None, 0), lay)
        out_p.append(st_p)
        past_s = (state_gdn_conv[:, l], state_gdn[:, l], state_retention[:, l], cache_nsa_window[:, l],
                  past_kv[l], past_len)
        hs, st_s = _layer(hs, pos_s, lw, l, past_s, lay)
        out_s.append(st_s)
    y_prompt = _rmsnorm(hp.reshape(bp * tp, d), norm_final, F32).reshape(bp, tp, d)
    y_sample = _rmsnorm(hs.reshape(bs * ts, d), norm_final, F32).reshape(bs, ts, d)
    stack = lambda outs, j, axis: jnp.stack([o[j] for o in outs], axis=axis)
    return (y_prompt, y_sample,
            stack(out_p, 0, 2), stack(out_s, 0, 2),
            stack(out_p, 2, 1), stack(out_s, 2, 1),
            stack(out_p, 1, 1), stack(out_s, 1, 1),
            stack(out_p, 3, 1), stack(out_s, 3, 1),
            stack(out_p, 4, 1), stack(out_s, 4, 1))
```
